```python
import math
import jax, jax.numpy as jnp
from jax import lax
import numpy as np

D_MODEL = 4096
BATCH = 2
SEQ = 8192
DEPTH = 2

N_A_LAYERS = DEPTH // 2
N_B_LAYERS = DEPTH - N_A_LAYERS

RWKV_HEAD_DIM = 64
RWKV_HEADS = D_MODEL // RWKV_HEAD_DIM
DECAY_LORA = max(32, int(round(1.8 * D_MODEL ** 0.5 / 32)) * 32)
ICLR_LORA = max(32, int(round(1.8 * D_MODEL ** 0.5 / 32)) * 32)
N_SHIFT_MIX = 6
GN_EPS = 64e-5

MOBA_HEAD_DIM = 128
MOBA_HEADS = D_MODEL // MOBA_HEAD_DIM
MOBA_BLOCK = 256
MOBA_TOPK = 3
Q_CHUNK = 32

NORM_EPS = 1e-6

kernel_name = "yoco_rwkv7_moba_hybrid"


def _rmsnorm(x, g):
    xf = x.astype(jnp.float32)
    y = xf * lax.rsqrt(jnp.mean(xf * xf, axis=-1, keepdims=True) + NORM_EPS)
    return (y * g.astype(jnp.float32)).astype(x.dtype)


def _rwkv7_scan(r, decay, k, v, kk, kka):
    B, T, H, N = r.shape

    def step(S, inp):
        r_t, w_t, k_t, v_t, kk_t, kka_t = inp
        sa = jnp.einsum('bhvk,bhk->bhv', S, -kk_t)
        S = (S * w_t[:, :, None, :] + sa[..., None] * kka_t[:, :, None, :]
             + v_t[..., None] * k_t[:, :, None, :])
        y_t = jnp.einsum('bhvk,bhk->bhv', S, r_t)
        return S, y_t

    xs = (jnp.moveaxis(r, 1, 0), jnp.moveaxis(decay, 1, 0), jnp.moveaxis(k, 1, 0),
          jnp.moveaxis(v, 1, 0), jnp.moveaxis(kk, 1, 0), jnp.moveaxis(kka, 1, 0))
    S0 = jnp.zeros((B, H, N, N), jnp.float32)
    _, y = lax.scan(step, S0, xs)
    return jnp.moveaxis(y, 0, 1)


def _rwkv7_mixer(hn, mu, w_in, w0, w1, w2, a0, a1, a2, k_k, k_a, r_k, lnx_w, lnx_b, w_o):
    f32 = jnp.float32
    B, T, D = hn.shape
    H, N = RWKV_HEADS, RWKV_HEAD_DIM
    dx = jnp.pad(hn, ((0, 0), (1, 0), (0, 0)))[:, :T] - hn

    def lerp(i):
        return hn + dx * mu[i]

    r = lerp(0) @ w_in[0]
    k = lerp(1) @ w_in[1]
    v = lerp(2) @ w_in[2]
    gate = lerp(3) @ w_in[3]
    wlog = -jax.nn.softplus(-(w0 + jnp.tanh(lerp(4) @ w1) @ w2).astype(f32)) - 0.5
    decay = jnp.exp(-jnp.exp(wlog))
    a = jax.nn.sigmoid((a0 + (lerp(5) @ a1) @ a2).astype(f32))

    r = r.astype(f32).reshape(B, T, H, N)
    k = k.astype(f32).reshape(B, T, H, N)
    v = v.astype(f32).reshape(B, T, H, N)
    a = a.reshape(B, T, H, N)
    decay = decay.reshape(B, T, H, N)

    kk = k * k_k.astype(f32).reshape(H, N)
    kk = kk / jnp.maximum(jnp.sqrt(jnp.sum(kk * kk, axis=-1, keepdims=True)), 1e-12)
    k = k * (1.0 + (a - 1.0) * k_a.astype(f32).reshape(H, N))

    y = _rwkv7_scan(r, decay, k, v, kk, kk * a)

    mean = jnp.mean(y, axis=-1, keepdims=True)
    var = jnp.mean(jnp.square(y - mean), axis=-1, keepdims=True)
    y = ((y - mean) * lax.rsqrt(var + GN_EPS)).reshape(B, T, D) * lnx_w.astype(f32) + lnx_b.astype(f32)
    bonus = jnp.sum(r * k * r_k.astype(f32), axis=-1, keepdims=True) * v
    y = (y + bonus.reshape(B, T, D)) * jax.nn.silu(gate.astype(f32))
    return y.astype(hn.dtype) @ w_o


def _shared_kv(h, kv_norm_g, w_k, w_v):
    B, T, D = h.shape
    H, Dh = MOBA_HEADS, MOBA_HEAD_DIM
    nb = -(-T // MOBA_BLOCK)
    pad = nb * MOBA_BLOCK - T
    hn = _rmsnorm(h, kv_norm_g)

    def blocks(t):
        t = jnp.pad(t, ((0, 0), (0, pad), (0, 0)))
        return t.reshape(B, nb, MOBA_BLOCK, H, Dh).transpose(0, 3, 1, 2, 4)

    kb = blocks(hn @ w_k)
    vb = blocks(hn @ w_v)
    kmean = jnp.mean(kb.astype(jnp.float32), axis=3).astype(kb.dtype)
    return kb, vb, kmean


def _moba_attend(q, kb, vb, kmean):
    f32 = jnp.float32
    B, H, T, Dh = q.shape
    nb = kb.shape[2]
    n_sel = min(MOBA_TOPK, nb)
    scale = Dh ** -0.5
    slopes = jnp.exp2(-8.0 * jnp.arange(1, H + 1, dtype=f32) / H)
    b_idx = jnp.arange(B)[:, None, None, None]
    h_idx = jnp.arange(H)[None, :, None, None]
    offs = jnp.arange(MOBA_BLOCK)
    blk_ids = jnp.arange(nb)

    def chunk(c):
        t0 = c * Q_CHUNK
        qc = lax.dynamic_slice_in_dim(q, t0, Q_CHUNK, axis=2)
        tpos = t0 + jnp.arange(Q_CHUNK)
        own = t0 // MOBA_BLOCK
        gate = jnp.einsum('bhcd,bhnd->bhcn', qc, kmean).astype(f32)
        gate = jnp.where(blk_ids < own, gate, -jnp.inf)
        gval, sel = lax.top_k(gate, n_sel)
        valid = gval > -jnp.inf
        k_sel = kb[b_idx, h_idx, sel]
        v_sel = vb[b_idx, h_idx, sel]
        kpos = sel[..., None] * MOBA_BLOCK + offs
        dist = (tpos[None, None, :, None, None] - kpos).astype(f32)
        s_sel = (jnp.einsum('bhcd,bhcjsd->bhcjs', qc, k_sel).astype(f32) * scale
                 - slopes[None, :, None, None, None] * dist)
        s_sel = jnp.where(valid[..., None], s_sel, -jnp.inf).reshape(B, H, Q_CHUNK, n_sel * MOBA_BLOCK)
        k_own = lax.dynamic_index_in_dim(kb, own, axis=2, keepdims=False)
        v_own = lax.dynamic_index_in_dim(vb, own, axis=2, keepdims=False)
        kpos_own = own * MOBA_BLOCK + offs
        dist_own = (tpos[:, None] - kpos_own[None, :]).astype(f32)
        s_own = (jnp.einsum('bhcd,bhsd->bhcs', qc, k_own).astype(f32) * scale
                 - slopes[None, :, None, None] * dist_own[None, None])
        s_own = jnp.where((dist_own >= 0)[None, None], s_own, -jnp.inf)
        p = jax.nn.softmax(jnp.concatenate([s_sel, s_own], axis=-1), axis=-1)
        p_sel = p[..., :n_sel * MOBA_BLOCK].reshape(B, H, Q_CHUNK, n_sel, MOBA_BLOCK).astype(vb.dtype)
        p_own = p[..., n_sel * MOBA_BLOCK:].astype(vb.dtype)
        return (jnp.einsum('bhcjs,bhcjsd->bhcd', p_sel, v_sel)
                + jnp.einsum('bhcs,bhsd->bhcd', p_own, v_own))

    out = lax.map(chunk, jnp.arange(T // Q_CHUNK))
    return out.transpose(1, 0, 3, 2, 4).reshape(B, T, H * Dh)


def _moba_mixer(hn, w_qg, w_o, kb, vb, kmean):
    B, T, D = hn.shape
    qg = hn @ w_qg
    q = qg[..., :D].reshape(B, T, MOBA_HEADS, MOBA_HEAD_DIM).transpose(0, 2, 1, 3)
    gate = qg[..., D:]
    att = _moba_attend(q, kb, vb, kmean)
    y = att.astype(jnp.float32) * jax.nn.silu(gate.astype(jnp.float32))
    return y.astype(hn.dtype) @ w_o


def setup_inputs(seed: int = 0) -> dict:
    key = jax.random.key(seed)
    ks = jax.random.split(key, 24)
    f32 = jnp.float32
    D, NA, NBL = D_MODEL, N_A_LAYERS, N_B_LAYERS
    s = D ** -0.5

    def nrm(k, shape, scale):
        return scale * jax.random.normal(k, shape, f32)

    def gain(k, shape):
        return 1.0 + 0.02 * jax.random.normal(k, shape, f32)

    return {
        "x": jax.random.normal(ks[0], (BATCH, SEQ, D), f32),
        "a_pre_g": gain(ks[1], (NA, D)),
        "a_post_g": gain(ks[2], (NA, D)),
        "a_mu": jax.random.uniform(ks[3], (NA, N_SHIFT_MIX, D), f32),
        "a_w_in": nrm(ks[4], (NA, 4, D, D), s),
        "a_w0": jax.random.uniform(ks[5], (NA, D), f32, minval=-6.0, maxval=1.0),
        "a_w1": nrm(ks[6], (NA, D, DECAY_LORA), s),
        "a_w2": nrm(ks[7], (NA, DECAY_LORA, D), 0.5 * DECAY_LORA ** -0.5),
        "a_a0": nrm(ks[8], (NA, D), 0.1),
        "a_a1": nrm(ks[9], (NA, D, ICLR_LORA), s),
        "a_a2": nrm(ks[10], (NA, ICLR_LORA, D), 0.5 * ICLR_LORA ** -0.5),
        "a_k_k": 0.85 + nrm(ks[11], (NA, D), 0.05),
        "a_k_a": 1.0 + nrm(ks[12], (NA, D), 0.05),
        "a_r_k": nrm(ks[13], (NA, RWKV_HEADS, RWKV_HEAD_DIM), 0.1),
        "a_lnx_w": gain(ks[14], (NA, D)),
        "a_lnx_b": nrm(ks[15], (NA, D), 0.02),
        "a_w_o": nrm(ks[16], (NA, D, D), s),
        "kv_norm_g": gain(ks[17], (D,)),
        "w_k": nrm(ks[18], (D, D), s),
        "w_v": nrm(ks[19], (D, D), s),
        "b_pre_g": gain(ks[20], (NBL, D)),
        "b_post_g": gain(ks[21], (NBL, D)),
        "b_w_qg": nrm(ks[22], (NBL, D, 2 * D), s),
        "b_w_o": nrm(ks[23], (NBL, D, D), s),
    }


def reference(x, a_pre_g, a_post_g, a_mu, a_w_in, a_w0, a_w1, a_w2, a_a0, a_a1, a_a2,
              a_k_k, a_k_a, a_r_k, a_lnx_w, a_lnx_b, a_w_o, kv_norm_g, w_k, w_v,
              b_pre_g, b_post_g, b_w_qg, b_w_o):
    h = x
    kb = vb = kmean = None
    for layer in range(DEPTH):
        if layer < N_A_LAYERS:
            i = layer
            mix = _rwkv7_mixer(_rmsnorm(h, a_pre_g[i]), a_mu[i], a_w_in[i], a_w0[i], a_w1[i],
                               a_w2[i], a_a0[i], a_a1[i], a_a2[i], a_k_k[i], a_k_a[i],
                               a_r_k[i], a_lnx_w[i], a_lnx_b[i], a_w_o[i])
            h = h + _rmsnorm(mix, a_post_g[i])
        else:
            j = layer - N_A_LAYERS
            if j == 0:
                kb, vb, kmean = _shared_kv(h, kv_norm_g, w_k, w_v)
            mix = _moba_mixer(_rmsnorm(h, b_pre_g[j]), b_w_qg[j], b_w_o[j], kb, vb, kmean)
            h = h + _rmsnorm(mix, b_post_g[j])
    return h
```

```python
import functools

import jax
import jax.numpy as jnp
from jax import lax
from jax.experimental import pallas as pl
from jax.experimental.pallas import tpu as pltpu

RWKV_HEAD_DIM = 64
MOBA_HEAD_DIM = 128
MOBA_BLOCK = 256
MOBA_TOPK = 3
GN_EPS = 64e-5
NORM_EPS = 1e-6

LANES = 128
SCAN_CHUNK = RWKV_HEAD_DIM
HEADS_PER_PACK = LANES // RWKV_HEAD_DIM
SCAN_LANES = 512
VMEM_LIMIT = 56 * 1024 * 1024

F32 = jnp.float32
BF16 = jnp.bfloat16


def _tile(n, pref, align):
    if n <= pref:
        return n
    t = (pref // align) * align
    while t >= align:
        if n % t == 0:
            return t
        t -= align
    return n


def _params(sem):
    return pltpu.CompilerParams(dimension_semantics=sem, vmem_limit_bytes=VMEM_LIMIT)


def _rms(x, g):
    return x * lax.rsqrt(jnp.mean(x * x, axis=-1, keepdims=True) + NORM_EPS) * g


def _dot(a, b):
    return jnp.dot(a, b, preferred_element_type=F32)


def _dot_nt(a, b):
    return lax.dot_general(a, b, (((1,), (1,)), ((), ())), preferred_element_type=F32)


def _dot_tn(a, b):
    return lax.dot_general(a, b, (((0,), (0,)), ((), ())), preferred_element_type=F32)


def _split(x):
    hi = x.astype(BF16)
    lo = (x - hi.astype(F32)).astype(BF16)
    return hi, lo


def _prep_kernel(x_ref, xp_ref, g_ref, mu_ref, o_ref):
    i = pl.program_id(1)
    g = g_ref[...]
    hn = _rms(x_ref[0], g)
    prev = _rms(xp_ref[0, 7:8, :], g)
    prev = jnp.where(i == 0, 0.0, prev)
    row = lax.broadcasted_iota(jnp.int32, hn.shape, 0)
    shifted = jnp.where(row == 0, prev, pltpu.roll(hn, shift=1, axis=0))
    dx = shifted - hn
    for n in range(o_ref.shape[0]):
        o_ref[n, 0] = (hn + dx * mu_ref[n : n + 1, :]).astype(o_ref.dtype)


def _prep(x, g, mu):
    B, T, D = x.shape
    n_mix = mu.shape[0]
    tt = _tile(T, 128, 8)
    return pl.pallas_call(
        _prep_kernel,
        out_shape=jax.ShapeDtypeStruct((n_mix, B, T, D), BF16),
        grid=(B, T // tt),
        in_specs=[
            pl.BlockSpec((1, tt, D), lambda b, i: (b, i, 0)),
            pl.BlockSpec((1, 8, D), lambda b, i: (b, jnp.maximum(i * (tt // 8) - 1, 0), 0)),
            pl.BlockSpec((1, D), lambda b, i: (0, 0)),
            pl.BlockSpec((n_mix, D), lambda b, i: (0, 0)),
        ],
        out_specs=pl.BlockSpec((n_mix, 1, tt, D), lambda b, i: (0, b, i, 0)),
        compiler_params=_params(("parallel", "parallel")),
        name="rwkv_prep",
    )(x, x, g.reshape(1, D), mu)


def _mm_kernel(x_ref, w_ref, o_ref):
    o_ref[0] = _dot(x_ref[0], w_ref[0]).astype(o_ref.dtype)


def _matmul(x, w, out_dtype, *, x_off=0, name="matmul"):
    G, K, N = w.shape
    M = x.shape[1]
    tm = _tile(M, 1024, 8)
    tn = _tile(N, 512, LANES)
    return pl.pallas_call(
        _mm_kernel,
        out_shape=jax.ShapeDtypeStruct((G, M, N), out_dtype),
        grid=(G, M // tm, N // tn),
        in_specs=[
            pl.BlockSpec((1, tm, K), lambda g, i, j: (g + x_off, i, 0)),
            pl.BlockSpec((1, K, tn), lambda g, i, j: (g, 0, j)),
        ],
        out_specs=pl.BlockSpec((1, tm, tn), lambda g, i, j: (g, i, j)),
        compiler_params=_params(("parallel", "parallel", "arbitrary")),
        name=name,
    )(x, w)


def _rates_kernel(lo_ref, w0_ref, w2_ref, a0_ref, a2_ref, lw_ref, ai_ref):
    z = w0_ref[...] + _dot(jnp.tanh(lo_ref[0]).astype(BF16), w2_ref[...])
    u = -z
    softplus = jnp.maximum(u, 0.0) + jnp.log1p(jnp.exp(-jnp.abs(u)))
    wlog = -softplus - 0.5
    lw_ref[...] = -jnp.exp(wlog)
    y = a0_ref[...] + _dot(lo_ref[1].astype(BF16), a2_ref[...])
    ai_ref[...] = 1.0 / (1.0 + jnp.exp(-y))


def _rates(lo, w0, w2, a0, a2):
    _, M, R = lo.shape
    D = w2.shape[1]
    tm = _tile(M, 256, 8)
    row = pl.BlockSpec((1, D), lambda i: (0, 0))
    mat = pl.BlockSpec((R, D), lambda i: (0, 0))
    out = pl.BlockSpec((tm, D), lambda i: (i, 0))
    return pl.pallas_call(
        _rates_kernel,
        out_shape=[jax.ShapeDtypeStruct((M, D), F32)] * 2,
        grid=(M // tm,),
        in_specs=[pl.BlockSpec((2, tm, R), lambda i: (0, i, 0)), row, mat, row, mat],
        out_specs=[out, out],
        compiler_params=_params(("parallel",)),
        name="rwkv_rates",
    )(lo, w0.reshape(1, D), w2, a0.reshape(1, D), a2)


def _scan_kernel(r_ref, k_ref, v_ref, g_ref, lw_ref, ai_ref, kk_ref, ka_ref, rk_ref,
                 lnw_ref, lnb_ref, o_ref, h_ref):
    C = SCAN_CHUNK
    N = RWKV_HEAD_DIM
    L = LANES
    HP = HEADS_PER_PACK
    n_packs = o_ref.shape[2] // L

    @pl.when(pl.program_id(2) == 0)
    def _():
        h_ref[...] = jnp.zeros_like(h_ref)

    t_row = lax.broadcasted_iota(jnp.int32, (C, HP * C), 0)
    s_col = lax.broadcasted_iota(jnp.int32, (C, HP * C), 1) % C
    strict = s_col < t_row
    incl = s_col <= t_row
    eye_packed = (s_col == t_row).astype(F32)
    tri = (lax.broadcasted_iota(jnp.int32, (C, C), 1)
           <= lax.broadcasted_iota(jnp.int32, (C, C), 0)).astype(BF16)
    lr = lax.broadcasted_iota(jnp.int32, (L, L), 0)
    lc = lax.broadcasted_iota(jnp.int32, (L, L), 1)
    same_head = (lr // N) == (lc // N)
    ones_bd = same_head.astype(BF16)
    eye_l = (lr == lc).astype(F32)
    lane_head = lax.broadcasted_iota(jnp.int32, (C, L), 1) // N

    def bd(x):
        return jnp.concatenate(
            [jnp.where(lane_head == h, x, jnp.zeros_like(x)) for h in range(HP)], axis=0)

    def seg_sum(x):
        hi, lo = _split(x)
        return _dot(hi, ones_bd) + _dot(lo, ones_bd)

    for p in range(n_packs):
        sl = slice(p * L, (p + 1) * L)
        r = r_ref[0, 0, :, sl]
        k = k_ref[0, 0, :, sl]
        v = v_ref[0, 0, :, sl]
        gate = g_ref[0, 0, :, sl]
        lw = lw_ref[0, :, sl]
        ai = ai_ref[0, :, sl]

        kk = k * kk_ref[:, sl]
        kk = kk / jnp.maximum(jnp.sqrt(seg_sum(kk * kk)), 1e-12)
        k2 = k * (1.0 + (ai - 1.0) * ka_ref[:, sl])
        a = -kk
        b = kk * ai

        lw_hi, lw_lo = _split(lw)
        cum = _dot(tri, lw_hi) + _dot(tri, lw_lo)
        cum_last = cum[C - 1 : C, :]
        g_inv = jnp.exp(-cum)
        at = a * jnp.exp(cum - lw)
        rt = r * jnp.exp(cum)
        bt = b * g_inv
        kt = k2 * g_inv
        g_tail = jnp.exp(cum_last - cum)
        bh = b * g_tail
        kh = k2 * g_tail
        g_col = jnp.sum(eye_l * jnp.exp(cum_last), axis=1, keepdims=True)

        lhs = jnp.concatenate([at, rt], axis=0).astype(BF16)
        v16 = v.astype(BF16)
        at16 = at.astype(BF16)
        bdv = bd(v16)
        sb = _dot_nt(lhs, bd(bt.astype(BF16)))
        sk = _dot_nt(lhs, bd(kt.astype(BF16)))
        a_ab = jnp.where(strict, sb[:C], 0.0)
        a_rb = jnp.where(incl, sb[C:], 0.0)
        a_ak = jnp.where(strict, sk[:C], 0.0)
        a_rk = jnp.where(incl, sk[C:], 0.0)

        mpow = a_ab
        tinv = eye_packed + a_ab
        n_sq = 1
        while n_sq * 2 < C:
            m16 = mpow.astype(BF16)
            mpow = _dot(m16, bd(m16))
            tinv = tinv + _dot(tinv.astype(BF16), bd(mpow.astype(BF16)))
            n_sq *= 2
        t16 = tinv.astype(BF16)

        x1 = _dot(a_ak.astype(BF16), bdv)
        ua = _dot(t16, jnp.concatenate([bd(x1.astype(BF16)), bd(at16)], axis=1))
        u0 = ua[:, :L].astype(BF16)
        atp = ua[:, L:].astype(BF16)
        ry = _dot(a_rb.astype(BF16), jnp.concatenate([bd(atp), bd(u0)], axis=1))
        rp = rt + ry[:, :L]
        y0 = ry[:, L:] + _dot(a_rk.astype(BF16), bdv)

        pq = _dot_tn(bh.astype(BF16), jnp.concatenate([atp, u0], axis=1))
        pm = jnp.where(same_head, pq[:, :L], 0.0)
        qm = jnp.where(same_head, pq[:, L:] + _dot_tn(kh.astype(BF16), v16), 0.0)

        h0 = h_ref[p]
        h16 = h0.astype(BF16)
        y = _dot(rp.astype(BF16), h16) + y0
        h_ref[p] = g_col * h0 + _dot(pm.astype(BF16), h16) + qm

        mean = seg_sum(y) * (1.0 / N)
        yc = y - mean
        var = seg_sum(yc * yc) * (1.0 / N)
        yn = yc * lax.rsqrt(var + GN_EPS) * lnw_ref[:, sl] + lnb_ref[:, sl]
        bonus = seg_sum(r * k2 * rk_ref[:, sl]) * v
        out = (yn + bonus) * (gate / (1.0 + jnp.exp(-gate)))
        o_ref[0, :, sl] = out.astype(o_ref.dtype)


def _scan(rkvg, lw, ai, k_k, k_a, r_k, lnx_w, lnx_b):
    _, B, T, D = rkvg.shape
    C = SCAN_CHUNK
    W = _tile(D, SCAN_LANES, LANES)
    act = lambda n: pl.BlockSpec((1, 1, C, W), lambda b, g, c, n=n: (n, b, c, g))
    seq = pl.BlockSpec((1, C, W), lambda b, g, c: (b, c, g))
    row = pl.BlockSpec((1, W), lambda b, g, c: (0, g))
    return pl.pallas_call(
        _scan_kernel,
        out_shape=jax.ShapeDtypeStruct((B, T, D), BF16),
        grid=(B, D // W, T // C),
        in_specs=[act(0), act(1), act(2), act(3), seq, seq, row, row, row, row, row],
        out_specs=seq,
        scratch_shapes=[pltpu.VMEM((W // LANES, LANES, LANES), F32)],
        compiler_params=_params(("parallel", "parallel", "arbitrary")),
        name="rwkv_scan",
    )(rkvg, rkvg, rkvg, rkvg, lw, ai, k_k.reshape(1, D), k_a.reshape(1, D), r_k.reshape(1, D),
      lnx_w.reshape(1, D), lnx_b.reshape(1, D))


def _res_a_kernel(x_ref, mix_ref, gp_ref, gkv_ref, gb_ref, h_ref, hkv_ref, hb_ref):
    h = x_ref[...] + _rms(mix_ref[...], gp_ref[...])
    h_ref[...] = h
    hkv_ref[...] = _rms(h, gkv_ref[...]).astype(hkv_ref.dtype)
    hb_ref[...] = _rms(h, gb_ref[...]).astype(hb_ref.dtype)


def _res_a(x, mix, g_post, g_kv, g_b):
    M, D = x.shape
    tm = _tile(M, 128, 8)
    blk = pl.BlockSpec((tm, D), lambda i: (i, 0))
    row = pl.BlockSpec((1, D), lambda i: (0, 0))
    return pl.pallas_call(
        _res_a_kernel,
        out_shape=[jax.ShapeDtypeStruct((M, D), F32), jax.ShapeDtypeStruct((M, D), BF16),
                   jax.ShapeDtypeStruct((M, D), BF16)],
        grid=(M // tm,),
        in_specs=[blk, blk, row, row, row],
        out_specs=[blk, blk, blk],
        compiler_params=_params(("parallel",)),
        name="residual_a",
    )(x, mix, g_post.reshape(1, D), g_kv.reshape(1, D), g_b.reshape(1, D))


def _res_b_kernel(h_ref, mix_ref, g_ref, o_ref):
    o_ref[...] = h_ref[...] + _rms(mix_ref[...], g_ref[...])


def _res_b(h, mix, g):
    M, D = h.shape
    tm = _tile(M, 256, 8)
    blk = pl.BlockSpec((tm, D), lambda i: (i, 0))
    return pl.pallas_call(
        _res_b_kernel,
        out_shape=jax.ShapeDtypeStruct((M, D), F32),
        grid=(M // tm,),
        in_specs=[blk, blk, pl.BlockSpec((1, D), lambda i: (0, 0))],
        out_specs=blk,
        compiler_params=_params(("parallel",)),
        name="residual_b",
    )(h, mix, g.reshape(1, D))


def _attn_kernel(q_ref, gate_ref, k_ref, v_ref, o_ref, kb_ref, vt_ref, km_ref, sel_ref, *, n_heads):
    BS = MOBA_BLOCK
    T = k_ref.shape[1]
    NB = T // BS
    h = pl.program_id(1)
    i = pl.program_id(2)

    @pl.when(i == 0)
    def _():
        kb_ref[...] = k_ref[0].astype(BF16)
        for j in range(NB):
            rows = slice(j * BS, (j + 1) * BS)
            vt_ref[j] = v_ref[0, rows, :].T.astype(BF16)
            km_ref[j : j + 1, :] = jnp.mean(k_ref[0, rows, :], axis=0, keepdims=True)

    q_t = q_ref[0].T
    q16 = q_t.astype(BF16)

    gate_s = lax.dot_general(km_ref[...], q_t, (((1,), (0,)), ((), ())),
                             precision=lax.Precision.HIGHEST, preferred_element_type=F32)
    blk = lax.broadcasted_iota(jnp.int32, gate_s.shape, 0)
    neg_inf = jnp.float32(-jnp.inf)
    gate_s = jnp.where(blk < i, gate_s, neg_inf)
    sel = jnp.zeros(gate_s.shape, F32)
    for _ in range(min(MOBA_TOPK, NB)):
        top = jnp.max(gate_s, axis=0, keepdims=True)
        first = jnp.min(jnp.where(gate_s == top, blk, NB), axis=0, keepdims=True)
        hit = blk == first
        sel = jnp.where(hit & (top > neg_inf), 1.0, sel)
        gate_s = jnp.where(hit, neg_inf, gate_s)
    sel_ref[...] = sel

    h_row = jnp.zeros((1, BS), F32) + (h + 1).astype(F32)
    slope = jnp.exp2(-8.0 * h_row / n_heads)
    scale = MOBA_HEAD_DIM ** -0.5
    d0 = (lax.broadcasted_iota(jnp.int32, (BS, BS), 1)
          - lax.broadcasted_iota(jnp.int32, (BS, BS), 0)).astype(F32)

    def update(j, keep, carry):
        m, l, acc = carry
        start = pl.multiple_of(j * BS, BS)
        s = _dot(kb_ref[pl.ds(start, BS), :], q16)
        dist = d0 + ((i - j) * BS).astype(F32)
        s = s * scale - slope * dist
        s = jnp.where((dist >= 0.0) & (keep > 0.0), s, neg_inf)
        m_new = jnp.maximum(m, jnp.max(s, axis=0, keepdims=True))
        alpha = jnp.exp(m - m_new)
        p = jnp.exp(s - m_new)
        l = l * alpha + jnp.sum(p, axis=0, keepdims=True)
        acc = acc * alpha + _dot(vt_ref[j], p.astype(BF16))
        return m_new, l, acc

    init = (jnp.full((1, BS), neg_inf, F32), jnp.zeros((1, BS), F32),
            jnp.zeros((MOBA_HEAD_DIM, BS), F32))
    carry = update(i, jnp.ones((1, BS), F32), init)
    m, l, acc = lax.fori_loop(
        0, i, lambda j, c: update(j, sel_ref[pl.ds(j, 1), :], c), carry)

    att = (acc / l).T
    gate = gate_ref[0]
    o_ref[0] = (att * (gate / (1.0 + jnp.exp(-gate)))).astype(o_ref.dtype)


def _attention(qg, kv, n_heads):
    B, T, D2 = qg.shape
    D = D2 // 2
    BS, Dh = MOBA_BLOCK, MOBA_HEAD_DIM
    NB = T // BS
    return pl.pallas_call(
        functools.partial(_attn_kernel, n_heads=n_heads),
        out_shape=jax.ShapeDtypeStruct((B, T, D), BF16),
        grid=(B, n_heads, NB),
        in_specs=[
            pl.BlockSpec((1, BS, Dh), lambda b, h, i: (b, i, h)),
            pl.BlockSpec((1, BS, Dh), lambda b, h, i: (b, i, n_heads + h)),
            pl.BlockSpec((1, T, Dh), lambda b, h, i: (b, 0, h)),
            pl.BlockSpec((1, T, Dh), lambda b, h, i: (b, 0, n_heads + h)),
        ],
        out_specs=pl.BlockSpec((1, BS, Dh), lambda b, h, i: (b, i, h)),
        scratch_shapes=[
            pltpu.VMEM((T, Dh), BF16),
            pltpu.VMEM((NB, Dh, BS), BF16),
            pltpu.VMEM((NB, Dh), F32),
            pltpu.VMEM((NB, BS), F32),
        ],
        compiler_params=_params(("parallel", "parallel", "arbitrary")),
        name="moba_attention",
    )(qg, qg, kv, kv)


def kernel(x, a_pre_g, a_post_g, a_mu, a_w_in, a_w0, a_w1, a_w2, a_a0, a_a1, a_a2, a_k_k, a_k_a,
           a_r_k, a_lnx_w, a_lnx_b, a_w_o, kv_norm_g, w_k, w_v, b_pre_g, b_post_g, b_w_qg, b_w_o):
    B, T, D = x.shape
    M = B * T
    assert a_pre_g.shape[0] == 1 and b_pre_g.shape[0] == 1, "one RWKV layer then one MoBA layer"
    assert T % MOBA_BLOCK == 0 and D % LANES == 0 and T % SCAN_CHUNK == 0
    n_moba_heads = D // MOBA_HEAD_DIM

    mixes = _prep(x, a_pre_g[0], a_mu[0]).reshape(a_mu.shape[1], M, D)
    rkvg = _matmul(mixes, a_w_in[0].astype(BF16), F32, name="rwkv_proj")
    w_lora = jnp.stack([a_w1[0], a_a1[0]]).astype(BF16)
    lo = _matmul(mixes, w_lora, F32, x_off=4, name="rwkv_lora")
    lw, ai = _rates(lo, a_w0[0], a_w2[0].astype(BF16), a_a0[0], a_a2[0].astype(BF16))
    yg = _scan(rkvg.reshape(4, B, T, D), lw.reshape(B, T, D), ai.reshape(B, T, D), a_k_k[0],
               a_k_a[0], a_r_k[0], a_lnx_w[0], a_lnx_b[0])
    mix_a = _matmul(yg.reshape(1, M, D), a_w_o[0].astype(BF16)[None], F32, name="rwkv_out")[0]
    h1, hkv, hb = _res_a(x.reshape(M, D), mix_a, a_post_g[0], kv_norm_g, b_pre_g[0])

    w_kv = jnp.concatenate([w_k, w_v], axis=1).astype(BF16)[None]
    kv = _matmul(hkv[None], w_kv, F32, name="kv_proj")[0]
    qg = _matmul(hb[None], b_w_qg[0].astype(BF16)[None], F32, name="qg_proj")[0]
    yb = _attention(qg.reshape(B, T, 2 * D), kv.reshape(B, T, 2 * D), n_moba_heads)
    mix_b = _matmul(yb.reshape(1, M, D), b_w_o[0].astype(BF16)[None], F32, name="moba_out")[0]
    return _res_b(h1, mix_b, b_post_g[0]).reshape(B, T, D)
```

```python
import functools

import jax
import jax.numpy as jnp
from jax import lax
from jax.experimental import pallas as pl
from jax.experimental.pallas import tpu as pltpu

RWKV_HEAD_DIM = 64
MOBA_HEAD_DIM = 128
MOBA_BLOCK = 256
MOBA_TOPK = 3
GN_EPS = 64e-5
NORM_EPS = 1e-6

LANES = 128
MXU_WIDTH = 256
SCAN_CHUNK = RWKV_HEAD_DIM
PACK_LANES = MXU_WIDTH
HEADS_PER_PACK = PACK_LANES // RWKV_HEAD_DIM
SCAN_LANES = 2048
ATTN_GROUP = 4
VMEM_LIMIT = 56 * 1024 * 1024

F32 = jnp.float32
BF16 = jnp.bfloat16


def _tile(n, pref, align):
    if n <= pref:
        return n
    t = (pref // align) * align
    while t >= align:
        if n % t == 0:
            return t
        t -= align
    return n


def _params(sem):
    return pltpu.CompilerParams(dimension_semantics=sem, vmem_limit_bytes=VMEM_LIMIT)


def _rms(x, g):
    return x * lax.rsqrt(jnp.mean(x * x, axis=-1, keepdims=True) + NORM_EPS) * g


def _dot(a, b):
    return jnp.dot(a, b, preferred_element_type=F32)


def _dot_nt(a, b):
    return lax.dot_general(a, b, (((1,), (1,)), ((), ())), preferred_element_type=F32)


def _dot_tn(a, b):
    return lax.dot_general(a, b, (((0,), (0,)), ((), ())), preferred_element_type=F32)


def _split(x):
    hi = x.astype(BF16)
    lo = (x - hi.astype(F32)).astype(BF16)
    return hi, lo


def _prep_kernel(x_ref, xp_ref, g_ref, mu_ref, o_ref):
    i = pl.program_id(1)
    g = g_ref[...]
    hn = _rms(x_ref[0], g)
    prev = _rms(xp_ref[0, 7:8, :], g)
    prev = jnp.where(i == 0, 0.0, prev)
    row = lax.broadcasted_iota(jnp.int32, hn.shape, 0)
    shifted = jnp.where(row == 0, prev, pltpu.roll(hn, shift=1, axis=0))
    dx = shifted - hn
    for n in range(o_ref.shape[0]):
        o_ref[n, 0] = (hn + dx * mu_ref[n : n + 1, :]).astype(o_ref.dtype)


def _prep(x, g, mu):
    B, T, D = x.shape
    n_mix = mu.shape[0]
    tt = _tile(T, 128, 8)
    return pl.pallas_call(
        _prep_kernel,
        out_shape=jax.ShapeDtypeStruct((n_mix, B, T, D), BF16),
        grid=(B, T // tt),
        in_specs=[
            pl.BlockSpec((1, tt, D), lambda b, i: (b, i, 0)),
            pl.BlockSpec((1, 8, D), lambda b, i: (b, jnp.maximum(i * (tt // 8) - 1, 0), 0)),
            pl.BlockSpec((1, D), lambda b, i: (0, 0)),
            pl.BlockSpec((n_mix, D), lambda b, i: (0, 0)),
        ],
        out_specs=pl.BlockSpec((n_mix, 1, tt, D), lambda b, i: (0, b, i, 0)),
        compiler_params=_params(("parallel", "parallel")),
        name="rwkv_prep",
    )(x, x, g.reshape(1, D), mu)


def _mm_kernel(x_ref, w_ref, o_ref):
    o_ref[0] = _dot(x_ref[0], w_ref[0]).astype(o_ref.dtype)


def _matmul(x, w, out_dtype, *, x_off=0, name="matmul"):
    G, K, N = w.shape
    M = x.shape[1]
    tm = _tile(M, 1024, 8)
    tn = _tile(N, 512, LANES)
    return pl.pallas_call(
        _mm_kernel,
        out_shape=jax.ShapeDtypeStruct((G, M, N), out_dtype),
        grid=(G, M // tm, N // tn),
        in_specs=[
            pl.BlockSpec((1, tm, K), lambda g, i, j: (g + x_off, i, 0)),
            pl.BlockSpec((1, K, tn), lambda g, i, j: (g, 0, j)),
        ],
        out_specs=pl.BlockSpec((1, tm, tn), lambda g, i, j: (g, i, j)),
        compiler_params=_params(("parallel", "parallel", "arbitrary")),
        name=name,
    )(x, w)


def _rates_kernel(lo_ref, w0_ref, w2_ref, a0_ref, a2_ref, lw_ref, ai_ref):
    z = w0_ref[...] + _dot(jnp.tanh(lo_ref[0]).astype(BF16), w2_ref[...])
    u = -z
    softplus = jnp.maximum(u, 0.0) + jnp.log1p(jnp.exp(-jnp.abs(u)))
    wlog = -softplus - 0.5
    lw_ref[...] = -jnp.exp(wlog)
    y = a0_ref[...] + _dot(lo_ref[1].astype(BF16), a2_ref[...])
    ai_ref[...] = 1.0 / (1.0 + jnp.exp(-y))


def _rates(lo, w0, w2, a0, a2):
    _, M, R = lo.shape
    D = w2.shape[1]
    tm = _tile(M, 256, 8)
    row = pl.BlockSpec((1, D), lambda i: (0, 0))
    mat = pl.BlockSpec((R, D), lambda i: (0, 0))
    out = pl.BlockSpec((tm, D), lambda i: (i, 0))
    return pl.pallas_call(
        _rates_kernel,
        out_shape=[jax.ShapeDtypeStruct((M, D), F32)] * 2,
        grid=(M // tm,),
        in_specs=[pl.BlockSpec((2, tm, R), lambda i: (0, i, 0)), row, mat, row, mat],
        out_specs=[out, out],
        compiler_params=_params(("parallel",)),
        name="rwkv_rates",
    )(lo, w0.reshape(1, D), w2, a0.reshape(1, D), a2)


def _scan_kernel(r_ref, k_ref, v_ref, g_ref, lw_ref, ai_ref, kk_ref, ka_ref, rk_ref,
                 lnw_ref, lnb_ref, o_ref, s_ref):
    C = SCAN_CHUNK
    N = RWKV_HEAD_DIM
    L = PACK_LANES
    HP = HEADS_PER_PACK
    packs = range(o_ref.shape[2] // L)
    lanes = [slice(p * L, (p + 1) * L) for p in packs]

    @pl.when(pl.program_id(2) == 0)
    def _():
        s_ref[...] = jnp.zeros_like(s_ref)

    t_row = lax.broadcasted_iota(jnp.int32, (C, HP * C), 0)
    s_col = lax.broadcasted_iota(jnp.int32, (C, HP * C), 1) % C
    strict = s_col < t_row
    incl = s_col <= t_row
    eye_packed = (s_col == t_row).astype(F32)
    tri = (lax.broadcasted_iota(jnp.int32, (C, C), 1)
           <= lax.broadcasted_iota(jnp.int32, (C, C), 0)).astype(BF16)
    same_head = ((lax.broadcasted_iota(jnp.int32, (L, L), 0) // N)
                 == (lax.broadcasted_iota(jnp.int32, (L, L), 1) // N))
    ones_bd = same_head.astype(BF16)
    lane_head = lax.broadcasted_iota(jnp.int32, (C, L), 1) // N

    def bd(x):
        return jnp.concatenate(
            [jnp.where(lane_head == h, x, jnp.zeros_like(x)) for h in range(HP)], axis=0)

    def seg_sums(xs):
        parts = []
        for x in xs:
            parts.extend(_split(x))
        res = _dot(jnp.concatenate(parts, axis=0), ones_bd)
        return [res[2 * n * C:(2 * n + 1) * C] + res[(2 * n + 1) * C:(2 * n + 2) * C]
                for n in range(len(xs))]

    r = [r_ref[0, 0, :, s] for s in lanes]
    k = [k_ref[0, 0, :, s] for s in lanes]
    v = [v_ref[0, 0, :, s] for s in lanes]
    lw = [lw_ref[0, :, s] for s in lanes]
    ai = [ai_ref[0, :, s] for s in lanes]

    kk = [k[p] * kk_ref[:, lanes[p]] for p in packs]
    ss = [seg_sums([kk[p] * kk[p]])[0] for p in packs]
    cum = []
    for p in packs:
        hi, lo = _split(lw[p])
        both = _dot(tri, jnp.concatenate([hi, lo], axis=1))
        cum.append(both[:, :L] + both[:, L:])

    k2, at, rt, bh, kh, g_last, v16, lhs, rhs = [], [], [], [], [], [], [], [], []
    for p in packs:
        kkn = kk[p] / jnp.maximum(jnp.sqrt(ss[p]), 1e-12)
        k2.append(k[p] * (1.0 + (ai[p] - 1.0) * ka_ref[:, lanes[p]]))
        b = kkn * ai[p]
        cum_last = cum[p][C - 1:C, :]
        g_inv = jnp.exp(-cum[p])
        g_tail = jnp.exp(cum_last - cum[p])
        at.append((-kkn * jnp.exp(cum[p] - lw[p])).astype(BF16))
        rt.append(r[p] * jnp.exp(cum[p]))
        bh.append((b * g_tail).astype(BF16))
        kh.append((k2[p] * g_tail).astype(BF16))
        g_last.append(jnp.exp(cum_last))
        v16.append(v[p].astype(BF16))
        lhs.append(jnp.concatenate([at[p], rt[p].astype(BF16)], axis=0))
        rhs.append(jnp.concatenate([bd((b * g_inv).astype(BF16)),
                                    bd((k2[p] * g_inv).astype(BF16))], axis=0))

    sbk = [_dot_nt(lhs[p], rhs[p]) for p in packs]
    a_ab = [jnp.where(strict, sbk[p][:C, :HP * C], 0.0) for p in packs]
    a_ak = [jnp.where(strict, sbk[p][:C, HP * C:], 0.0).astype(BF16) for p in packs]
    a_r = [jnp.concatenate([jnp.where(incl, sbk[p][C:, :HP * C], 0.0),
                            jnp.where(incl, sbk[p][C:, HP * C:], 0.0)], axis=1).astype(BF16)
           for p in packs]
    bdv = [bd(v16[p]) for p in packs]
    x1 = [_dot(a_ak[p], bdv[p]) for p in packs]

    mpow = a_ab
    tinv = [eye_packed + a_ab[p] for p in packs]
    n_sq = 1
    while n_sq * 2 < C:
        m16 = [mpow[p].astype(BF16) for p in packs]
        mpow = [_dot(m16[p], bd(m16[p])) for p in packs]
        step = [_dot(tinv[p].astype(BF16), bd(mpow[p].astype(BF16))) for p in packs]
        tinv = [tinv[p] + step[p] for p in packs]
        n_sq *= 2

    ua = [_dot(tinv[p].astype(BF16),
               jnp.concatenate([bd(x1[p].astype(BF16)), bd(at[p])], axis=1)) for p in packs]
    u0 = [ua[p][:, :L].astype(BF16) for p in packs]
    atp = [ua[p][:, L:].astype(BF16) for p in packs]
    zeros = jnp.zeros((HP * C, L), BF16)
    ry = [_dot(a_r[p], jnp.concatenate(
        [jnp.concatenate([bd(atp[p]), bd(u0[p])], axis=1),
         jnp.concatenate([zeros, bdv[p]], axis=1)], axis=0)) for p in packs]
    rp = [(rt[p] + ry[p][:, :L]).astype(BF16) for p in packs]
    y0 = [ry[p][:, L:] for p in packs]

    zc = jnp.zeros((C, L), BF16)
    pq = [_dot_tn(jnp.concatenate([jnp.concatenate([atp[p], u0[p]], axis=1),
                                   jnp.concatenate([zc, v16[p]], axis=1)], axis=0),
                  jnp.concatenate([bh[p], kh[p]], axis=0)) for p in packs]
    pt = [jnp.where(same_head, pq[p][:L], 0.0).astype(BF16) for p in packs]
    qt = [jnp.where(same_head, pq[p][L:], 0.0) for p in packs]

    s0 = [s_ref[p] for p in packs]
    s16 = [s0[p].astype(BF16) for p in packs]
    y = [_dot_nt(rp[p], s16[p]) + y0[p] for p in packs]
    for p in packs:
        s_ref[p] = s0[p] * g_last[p] + _dot(s16[p], pt[p]) + qt[p]

    st = [seg_sums([y[p], r[p] * k2[p] * rk_ref[:, lanes[p]]]) for p in packs]
    yc = [y[p] - st[p][0] * (1.0 / N) for p in packs]
    var = [seg_sums([yc[p] * yc[p]])[0] * (1.0 / N) for p in packs]
    for p in packs:
        yn = yc[p] * lax.rsqrt(var[p] + GN_EPS) * lnw_ref[:, lanes[p]] + lnb_ref[:, lanes[p]]
        gate = g_ref[0, 0, :, lanes[p]]
        out = (yn + st[p][1] * v[p]) * (gate / (1.0 + jnp.exp(-gate)))
        o_ref[0, :, lanes[p]] = out.astype(o_ref.dtype)


def _scan(rkvg, lw, ai, k_k, k_a, r_k, lnx_w, lnx_b):
    _, B, T, D = rkvg.shape
    C = SCAN_CHUNK
    W = _tile(D, SCAN_LANES, PACK_LANES)
    act = lambda n: pl.BlockSpec((1, 1, C, W), lambda b, g, c, n=n: (n, b, c, g))
    seq = pl.BlockSpec((1, C, W), lambda b, g, c: (b, c, g))
    row = pl.BlockSpec((1, W), lambda b, g, c: (0, g))
    return pl.pallas_call(
        _scan_kernel,
        out_shape=jax.ShapeDtypeStruct((B, T, D), BF16),
        grid=(B, D // W, T // C),
        in_specs=[act(0), act(1), act(2), act(3), seq, seq, row, row, row, row, row],
        out_specs=seq,
        scratch_shapes=[pltpu.VMEM((W // PACK_LANES, PACK_LANES, PACK_LANES), F32)],
        compiler_params=_params(("parallel", "parallel", "arbitrary")),
        name="rwkv_scan",
    )(rkvg, rkvg, rkvg, rkvg, lw, ai, k_k.reshape(1, D), k_a.reshape(1, D), r_k.reshape(1, D),
      lnx_w.reshape(1, D), lnx_b.reshape(1, D))


def _res_a_kernel(x_ref, mix_ref, gp_ref, gkv_ref, gb_ref, h_ref, hkv_ref, hb_ref):
    h = x_ref[...] + _rms(mix_ref[...], gp_ref[...])
    h_ref[...] = h
    hkv_ref[...] = _rms(h, gkv_ref[...]).astype(hkv_ref.dtype)
    hb_ref[...] = _rms(h, gb_ref[...]).astype(hb_ref.dtype)


def _res_a(x, mix, g_post, g_kv, g_b):
    M, D = x.shape
    tm = _tile(M, 128, 8)
    blk = pl.BlockSpec((tm, D), lambda i: (i, 0))
    row = pl.BlockSpec((1, D), lambda i: (0, 0))
    return pl.pallas_call(
        _res_a_kernel,
        out_shape=[jax.ShapeDtypeStruct((M, D), F32), jax.ShapeDtypeStruct((M, D), BF16),
                   jax.ShapeDtypeStruct((M, D), BF16)],
        grid=(M // tm,),
        in_specs=[blk, blk, row, row, row],
        out_specs=[blk, blk, blk],
        compiler_params=_params(("parallel",)),
        name="residual_a",
    )(x, mix, g_post.reshape(1, D), g_kv.reshape(1, D), g_b.reshape(1, D))


def _res_b_kernel(h_ref, mix_ref, g_ref, o_ref):
    o_ref[...] = h_ref[...] + _rms(mix_ref[...], g_ref[...])


def _res_b(h, mix, g):
    M, D = h.shape
    tm = _tile(M, 256, 8)
    blk = pl.BlockSpec((tm, D), lambda i: (i, 0))
    return pl.pallas_call(
        _res_b_kernel,
        out_shape=jax.ShapeDtypeStruct((M, D), F32),
        grid=(M // tm,),
        in_specs=[blk, blk, pl.BlockSpec((1, D), lambda i: (0, 0))],
        out_specs=blk,
        compiler_params=_params(("parallel",)),
        name="residual_b",
    )(h, mix, g.reshape(1, D))


def _attn_kernel(q_ref, gate_ref, k_ref, v_ref, o_ref, kb_ref, vt_ref, km_ref, sel_ref, bias_ref, *,
                 n_heads):
    BS = MOBA_BLOCK
    G = ATTN_GROUP
    assert G & (G - 1) == 0, "group size must be a power of two"
    T = k_ref.shape[1]
    NB = T // BS
    h = pl.program_id(1)
    i = pl.program_id(2)

    @pl.when(i == 0)
    def _():
        kb_ref[...] = k_ref[0].astype(BF16)
        for j in range(NB):
            rows = slice(j * BS, (j + 1) * BS)
            vt_ref[j] = v_ref[0, rows, :].T.astype(BF16)
            km_ref[j : j + 1, :] = jnp.mean(k_ref[0, rows, :], axis=0, keepdims=True)

    scale = MOBA_HEAD_DIM ** -0.5
    q_t = q_ref[0].T
    q16 = (q_t * scale).astype(BF16)

    gate_s = lax.dot_general(km_ref[...], q_t, (((1,), (0,)), ((), ())),
                             precision=lax.Precision.HIGHEST, preferred_element_type=F32)
    blk = lax.broadcasted_iota(jnp.int32, gate_s.shape, 0)
    neg_inf = jnp.float32(-jnp.inf)
    gate_s = jnp.where(blk < i, gate_s, neg_inf)
    sel = jnp.zeros(gate_s.shape, F32)
    for _ in range(min(MOBA_TOPK, NB)):
        top = jnp.max(gate_s, axis=0, keepdims=True)
        first = jnp.min(jnp.where(gate_s == top, blk, NB), axis=0, keepdims=True)
        hit = blk == first
        sel = jnp.where(hit & (top > neg_inf), 1.0, sel)
        gate_s = jnp.where(hit, neg_inf, gate_s)
    sel_ref[...] = sel

    h_row = jnp.zeros((1, BS), F32) + (h + 1).astype(F32)
    slope = jnp.exp2(-8.0 * h_row / n_heads)
    d0 = (lax.broadcasted_iota(jnp.int32, (BS, BS), 1)
          - lax.broadcasted_iota(jnp.int32, (BS, BS), 0)).astype(F32)
    bias_ref[...] = -slope * d0

    def scores(j):
        start = pl.multiple_of(j * BS, BS)
        return _dot(kb_ref[pl.ds(start, BS), :], q16)

    s = jnp.where(d0 >= 0.0, scores(i) + bias_ref[...], neg_inf)
    m = jnp.max(s, axis=0, keepdims=True)
    p = jnp.exp(s - m)
    l = jnp.sum(p, axis=0, keepdims=True)
    acc = _dot(vt_ref[i], p.astype(BF16))

    def group(g, carry):
        m, l, acc = carry
        blocks = [g * G + u for u in range(G)]
        idx = [jnp.minimum(jb, NB - 1) for jb in blocks]
        t = [scores(jc) for jc in idx]
        keep, shift, tops = [], [], []
        for u in range(G):
            in_past = (blocks[u] < i).astype(F32)
            keep.append(sel_ref[pl.ds(idx[u], 1), :] * in_past > 0.0)
            shift.append(slope * ((blocks[u] - i) * BS).astype(F32))
            t[u] = t[u] + bias_ref[...]
            top = jnp.max(t[u], axis=0, keepdims=True) + shift[u]
            tops.append(jnp.where(keep[u], top, neg_inf))
        m_new = m
        for top in tops:
            m_new = jnp.maximum(m_new, top)
        alpha = jnp.exp(m - m_new)
        l = l * alpha
        acc = acc * alpha
        for u in range(G):
            off = jnp.where(keep[u], m_new - shift[u], jnp.float32(jnp.inf))
            p = jnp.exp(t[u] - off)
            l = l + jnp.sum(p, axis=0, keepdims=True)
            acc = acc + _dot(vt_ref[idx[u]], p.astype(BF16))
        return m_new, l, acc

    n_groups = lax.shift_right_logical(i + (G - 1), jnp.int32(G.bit_length() - 1))
    m, l, acc = lax.fori_loop(0, n_groups, group, (m, l, acc))

    att = (acc / l).T
    gate = gate_ref[0]
    o_ref[0] = (att * (gate / (1.0 + jnp.exp(-gate)))).astype(o_ref.dtype)


def _attention(qg, kv, n_heads):
    B, T, D2 = qg.shape
    D = D2 // 2
    BS, Dh = MOBA_BLOCK, MOBA_HEAD_DIM
    NB = T // BS
    return pl.pallas_call(
        functools.partial(_attn_kernel, n_heads=n_heads),
        out_shape=jax.ShapeDtypeStruct((B, T, D), BF16),
        grid=(B, n_heads, NB),
        in_specs=[
            pl.BlockSpec((1, BS, Dh), lambda b, h, i: (b, i, h)),
            pl.BlockSpec((1, BS, Dh), lambda b, h, i: (b, i, n_heads + h)),
            pl.BlockSpec((1, T, Dh), lambda b, h, i: (b, 0, h)),
            pl.BlockSpec((1, T, Dh), lambda b, h, i: (b, 0, n_heads + h)),
        ],
        out_specs=pl.BlockSpec((1, BS, Dh), lambda b, h, i: (b, i, h)),
        scratch_shapes=[
            pltpu.VMEM((T, Dh), BF16),
            pltpu.VMEM((NB, Dh, BS), BF16),
            pltpu.VMEM((NB, Dh), F32),
            pltpu.VMEM((NB, BS), F32),
            pltpu.VMEM((BS, BS), F32),
        ],
        compiler_params=_params(("parallel", "parallel", "arbitrary")),
        name="moba_attention",
    )(qg, qg, kv, kv)


def kernel(x, a_pre_g, a_post_g, a_mu, a_w_in, a_w0, a_w1, a_w2, a_a0, a_a1, a_a2, a_k_k, a_k_a,
           a_r_k, a_lnx_w, a_lnx_b, a_w_o, kv_norm_g, w_k, w_v, b_pre_g, b_post_g, b_w_qg, b_w_o):
    B, T, D = x.shape
    M = B * T
    assert a_pre_g.shape[0] == 1 and b_pre_g.shape[0] == 1, "one RWKV layer then one MoBA layer"
    assert T % MOBA_BLOCK == 0 and D % LANES == 0 and T % SCAN_CHUNK == 0
    n_moba_heads = D // MOBA_HEAD_DIM

    mixes = _prep(x, a_pre_g[0], a_mu[0]).reshape(a_mu.shape[1], M, D)
    rkvg = _matmul(mixes, a_w_in[0].astype(BF16), F32, name="rwkv_proj")
    w_lora = jnp.stack([a_w1[0], a_a1[0]]).astype(BF16)
    lo = _matmul(mixes, w_lora, F32, x_off=4, name="rwkv_lora")
    lw, ai = _rates(lo, a_w0[0], a_w2[0].astype(BF16), a_a0[0], a_a2[0].astype(BF16))
    yg = _scan(rkvg.reshape(4, B, T, D), lw.reshape(B, T, D), ai.reshape(B, T, D), a_k_k[0],
               a_k_a[0], a_r_k[0], a_lnx_w[0], a_lnx_b[0])
    mix_a = _matmul(yg.reshape(1, M, D), a_w_o[0].astype(BF16)[None], F32, name="rwkv_out")[0]
    h1, hkv, hb = _res_a(x.reshape(M, D), mix_a, a_post_g[0], kv_norm_g, b_pre_g[0])

    w_kv = jnp.concatenate([w_k, w_v], axis=1).astype(BF16)[None]
    kv = _matmul(hkv[None], w_kv, F32, name="kv_proj")[0]
    qg = _matmul(hb[None], b_w_qg[0].astype(BF16)[None], F32, name="qg_proj")[0]
    yb = _attention(qg.reshape(B, T, 2 * D), kv.reshape(B, T, 2 * D), n_moba_heads)
    mix_b = _matmul(yb.reshape(1, M, D), b_w_o[0].astype(BF16)[None], F32, name="moba_out")[0]
    return _res_b(h1, mix_b, b_post_g[0]).reshape(B, T, D)
```

```python
import functools

import jax
import jax.numpy as jnp
from jax import lax
from jax.experimental import pallas as pl
from jax.experimental.pallas import tpu as pltpu

RWKV_HEAD_DIM = 64
MOBA_HEAD_DIM = 128
MOBA_BLOCK = 256
MOBA_TOPK = 3
GN_EPS = 64e-5
NORM_EPS = 1e-6

LANES = 128
MXU_WIDTH = 256
SCAN_CHUNK = RWKV_HEAD_DIM
PACK_LANES = MXU_WIDTH
HEADS_PER_PACK = PACK_LANES // RWKV_HEAD_DIM
SCAN_LANES = 2048
ATTN_GROUP = 2
VMEM_LIMIT = 56 * 1024 * 1024

F32 = jnp.float32
BF16 = jnp.bfloat16


def _tile(n, pref, align):
    if n <= pref:
        return n
    t = (pref // align) * align
    while t >= align:
        if n % t == 0:
            return t
        t -= align
    return n


def _params(sem):
    return pltpu.CompilerParams(dimension_semantics=sem, vmem_limit_bytes=VMEM_LIMIT)


def _rms(x, g):
    return x * lax.rsqrt(jnp.mean(x * x, axis=-1, keepdims=True) + NORM_EPS) * g


def _dot(a, b):
    return jnp.dot(a, b, preferred_element_type=F32)


def _dot_nt(a, b):
    return lax.dot_general(a, b, (((1,), (1,)), ((), ())), preferred_element_type=F32)


def _dot_tn(a, b):
    return lax.dot_general(a, b, (((0,), (0,)), ((), ())), preferred_element_type=F32)


def _split(x):
    hi = x.astype(BF16)
    lo = (x - hi.astype(F32)).astype(BF16)
    return hi, lo


def _prep_kernel(x_ref, xp_ref, g_ref, mu_ref, o_ref):
    i = pl.program_id(1)
    g = g_ref[...]
    hn = _rms(x_ref[0], g)
    prev = _rms(xp_ref[0, 7:8, :], g)
    prev = jnp.where(i == 0, 0.0, prev)
    row = lax.broadcasted_iota(jnp.int32, hn.shape, 0)
    shifted = jnp.where(row == 0, prev, pltpu.roll(hn, shift=1, axis=0))
    dx = shifted - hn
    for n in range(o_ref.shape[0]):
        o_ref[n, 0] = (hn + dx * mu_ref[n : n + 1, :]).astype(o_ref.dtype)


def _prep(x, g, mu):
    B, T, D = x.shape
    n_mix = mu.shape[0]
    tt = _tile(T, 128, 8)
    return pl.pallas_call(
        _prep_kernel,
        out_shape=jax.ShapeDtypeStruct((n_mix, B, T, D), BF16),
        grid=(B, T // tt),
        in_specs=[
            pl.BlockSpec((1, tt, D), lambda b, i: (b, i, 0)),
            pl.BlockSpec((1, 8, D), lambda b, i: (b, jnp.maximum(i * (tt // 8) - 1, 0), 0)),
            pl.BlockSpec((1, D), lambda b, i: (0, 0)),
            pl.BlockSpec((n_mix, D), lambda b, i: (0, 0)),
        ],
        out_specs=pl.BlockSpec((n_mix, 1, tt, D), lambda b, i: (0, b, i, 0)),
        compiler_params=_params(("parallel", "parallel")),
        name="rwkv_prep",
    )(x, x, g.reshape(1, D), mu)


def _mm_kernel(x_ref, w_ref, o_ref):
    o_ref[0] = _dot(x_ref[0], w_ref[0]).astype(o_ref.dtype)


def _matmul(x, w, out_dtype, *, x_off=0, name="matmul"):
    G, K, N = w.shape
    M = x.shape[1]
    tm = _tile(M, 1024, 8)
    tn = _tile(N, 512, LANES)
    return pl.pallas_call(
        _mm_kernel,
        out_shape=jax.ShapeDtypeStruct((G, M, N), out_dtype),
        grid=(G, M // tm, N // tn),
        in_specs=[
            pl.BlockSpec((1, tm, K), lambda g, i, j: (g + x_off, i, 0)),
            pl.BlockSpec((1, K, tn), lambda g, i, j: (g, 0, j)),
        ],
        out_specs=pl.BlockSpec((1, tm, tn), lambda g, i, j: (g, i, j)),
        compiler_params=_params(("parallel", "parallel", "arbitrary")),
        name=name,
    )(x, w)


def _rates_kernel(lo_ref, w0_ref, w2_ref, a0_ref, a2_ref, lw_ref, ai_ref):
    z = w0_ref[...] + _dot(jnp.tanh(lo_ref[0]).astype(BF16), w2_ref[...])
    u = -z
    softplus = jnp.maximum(u, 0.0) + jnp.log1p(jnp.exp(-jnp.abs(u)))
    wlog = -softplus - 0.5
    lw_ref[...] = -jnp.exp(wlog)
    y = a0_ref[...] + _dot(lo_ref[1].astype(BF16), a2_ref[...])
    ai_ref[...] = 1.0 / (1.0 + jnp.exp(-y))


def _rates(lo, w0, w2, a0, a2):
    _, M, R = lo.shape
    D = w2.shape[1]
    tm = _tile(M, 256, 8)
    row = pl.BlockSpec((1, D), lambda i: (0, 0))
    mat = pl.BlockSpec((R, D), lambda i: (0, 0))
    out = pl.BlockSpec((tm, D), lambda i: (i, 0))
    return pl.pallas_call(
        _rates_kernel,
        out_shape=[jax.ShapeDtypeStruct((M, D), F32)] * 2,
        grid=(M // tm,),
        in_specs=[pl.BlockSpec((2, tm, R), lambda i: (0, i, 0)), row, mat, row, mat],
        out_specs=[out, out],
        compiler_params=_params(("parallel",)),
        name="rwkv_rates",
    )(lo, w0.reshape(1, D), w2, a0.reshape(1, D), a2)


def _scan_kernel(r_ref, k_ref, v_ref, g_ref, lw_ref, ai_ref, kk_ref, ka_ref, rk_ref,
                 lnw_ref, lnb_ref, o_ref, s_ref):
    C = SCAN_CHUNK
    N = RWKV_HEAD_DIM
    L = PACK_LANES
    HP = HEADS_PER_PACK
    packs = range(o_ref.shape[2] // L)
    lanes = [slice(p * L, (p + 1) * L) for p in packs]

    @pl.when(pl.program_id(2) == 0)
    def _():
        s_ref[...] = jnp.zeros_like(s_ref)

    t_row = lax.broadcasted_iota(jnp.int32, (C, HP * C), 0)
    s_col = lax.broadcasted_iota(jnp.int32, (C, HP * C), 1) % C
    strict = s_col < t_row
    incl = s_col <= t_row
    eye_packed = (s_col == t_row).astype(F32)
    tri = (lax.broadcasted_iota(jnp.int32, (C, C), 1)
           <= lax.broadcasted_iota(jnp.int32, (C, C), 0)).astype(BF16)
    same_head = ((lax.broadcasted_iota(jnp.int32, (L, L), 0) // N)
                 == (lax.broadcasted_iota(jnp.int32, (L, L), 1) // N))
    ones_bd = same_head.astype(BF16)
    lane_head = lax.broadcasted_iota(jnp.int32, (C, L), 1) // N

    def bd(x):
        return jnp.concatenate(
            [jnp.where(lane_head == h, x, jnp.zeros_like(x)) for h in range(HP)], axis=0)

    def seg_sums(xs):
        parts = []
        for x in xs:
            parts.extend(_split(x))
        res = _dot(jnp.concatenate(parts, axis=0), ones_bd)
        return [res[2 * n * C:(2 * n + 1) * C] + res[(2 * n + 1) * C:(2 * n + 2) * C]
                for n in range(len(xs))]

    r = [r_ref[0, 0, :, s] for s in lanes]
    k = [k_ref[0, 0, :, s] for s in lanes]
    v = [v_ref[0, 0, :, s] for s in lanes]
    lw = [lw_ref[0, :, s] for s in lanes]
    ai = [ai_ref[0, :, s] for s in lanes]

    kk = [k[p] * kk_ref[:, lanes[p]] for p in packs]
    ss = [seg_sums([kk[p] * kk[p]])[0] for p in packs]
    cum = []
    for p in packs:
        hi, lo = _split(lw[p])
        both = _dot(tri, jnp.concatenate([hi, lo], axis=1))
        cum.append(both[:, :L] + both[:, L:])

    k2, at, rt, bh, kh, g_last, v16, lhs, rhs = [], [], [], [], [], [], [], [], []
    for p in packs:
        kkn = kk[p] / jnp.maximum(jnp.sqrt(ss[p]), 1e-12)
        k2.append(k[p] * (1.0 + (ai[p] - 1.0) * ka_ref[:, lanes[p]]))
        b = kkn * ai[p]
        cum_last = cum[p][C - 1:C, :]
        g_inv = jnp.exp(-cum[p])
        g_tail = jnp.exp(cum_last - cum[p])
        at.append((-kkn * jnp.exp(cum[p] - lw[p])).astype(BF16))
        rt.append(r[p] * jnp.exp(cum[p]))
        bh.append((b * g_tail).astype(BF16))
        kh.append((k2[p] * g_tail).astype(BF16))
        g_last.append(jnp.exp(cum_last))
        v16.append(v[p].astype(BF16))
        lhs.append(jnp.concatenate([at[p], rt[p].astype(BF16)], axis=0))
        rhs.append(jnp.concatenate([bd((b * g_inv).astype(BF16)),
                                    bd((k2[p] * g_inv).astype(BF16))], axis=0))

    sbk = [_dot_nt(lhs[p], rhs[p]) for p in packs]
    a_ab = [jnp.where(strict, sbk[p][:C, :HP * C], 0.0) for p in packs]
    a_ak = [jnp.where(strict, sbk[p][:C, HP * C:], 0.0).astype(BF16) for p in packs]
    a_r = [jnp.concatenate([jnp.where(incl, sbk[p][C:, :HP * C], 0.0),
                            jnp.where(incl, sbk[p][C:, HP * C:], 0.0)], axis=1).astype(BF16)
           for p in packs]
    bdv = [bd(v16[p]) for p in packs]
    x1 = [_dot(a_ak[p], bdv[p]) for p in packs]

    mpow = a_ab
    tinv = [eye_packed + a_ab[p] for p in packs]
    n_sq = 1
    while n_sq * 2 < C:
        m16 = [mpow[p].astype(BF16) for p in packs]
        mpow = [_dot(m16[p], bd(m16[p])) for p in packs]
        step = [_dot(tinv[p].astype(BF16), bd(mpow[p].astype(BF16))) for p in packs]
        tinv = [tinv[p] + step[p] for p in packs]
        n_sq *= 2

    ua = [_dot(tinv[p].astype(BF16),
               jnp.concatenate([bd(x1[p].astype(BF16)), bd(at[p])], axis=1)) for p in packs]
    u0 = [ua[p][:, :L].astype(BF16) for p in packs]
    atp = [ua[p][:, L:].astype(BF16) for p in packs]
    zeros = jnp.zeros((HP * C, L), BF16)
    ry = [_dot(a_r[p], jnp.concatenate(
        [jnp.concatenate([bd(atp[p]), bd(u0[p])], axis=1),
         jnp.concatenate([zeros, bdv[p]], axis=1)], axis=0)) for p in packs]
    rp = [(rt[p] + ry[p][:, :L]).astype(BF16) for p in packs]
    y0 = [ry[p][:, L:] for p in packs]

    zc = jnp.zeros((C, L), BF16)
    pq = [_dot_tn(jnp.concatenate([jnp.concatenate([atp[p], u0[p]], axis=1),
                                   jnp.concatenate([zc, v16[p]], axis=1)], axis=0),
                  jnp.concatenate([bh[p], kh[p]], axis=0)) for p in packs]
    pt = [jnp.where(same_head, pq[p][:L], 0.0).astype(BF16) for p in packs]
    qt = [jnp.where(same_head, pq[p][L:], 0.0) for p in packs]

    s0 = [s_ref[p] for p in packs]
    s16 = [s0[p].astype(BF16) for p in packs]
    y = [_dot_nt(rp[p], s16[p]) + y0[p] for p in packs]
    for p in packs:
        s_ref[p] = s0[p] * g_last[p] + _dot(s16[p], pt[p]) + qt[p]

    st = [seg_sums([y[p], r[p] * k2[p] * rk_ref[:, lanes[p]]]) for p in packs]
    yc = [y[p] - st[p][0] * (1.0 / N) for p in packs]
    var = [seg_sums([yc[p] * yc[p]])[0] * (1.0 / N) for p in packs]
    for p in packs:
        yn = yc[p] * lax.rsqrt(var[p] + GN_EPS) * lnw_ref[:, lanes[p]] + lnb_ref[:, lanes[p]]
        gate = g_ref[0, 0, :, lanes[p]]
        out = (yn + st[p][1] * v[p]) * (gate / (1.0 + jnp.exp(-gate)))
        o_ref[0, :, lanes[p]] = out.astype(o_ref.dtype)


def _scan(rkvg, lw, ai, k_k, k_a, r_k, lnx_w, lnx_b):
    _, B, T, D = rkvg.shape
    C = SCAN_CHUNK
    W = _tile(D, SCAN_LANES, PACK_LANES)
    act = lambda n: pl.BlockSpec((1, 1, C, W), lambda b, g, c, n=n: (n, b, c, g))
    seq = pl.BlockSpec((1, C, W), lambda b, g, c: (b, c, g))
    row = pl.BlockSpec((1, W), lambda b, g, c: (0, g))
    return pl.pallas_call(
        _scan_kernel,
        out_shape=jax.ShapeDtypeStruct((B, T, D), BF16),
        grid=(B, D // W, T // C),
        in_specs=[act(0), act(1), act(2), act(3), seq, seq, row, row, row, row, row],
        out_specs=seq,
        scratch_shapes=[pltpu.VMEM((W // PACK_LANES, PACK_LANES, PACK_LANES), F32)],
        compiler_params=_params(("parallel", "parallel", "arbitrary")),
        name="rwkv_scan",
    )(rkvg, rkvg, rkvg, rkvg, lw, ai, k_k.reshape(1, D), k_a.reshape(1, D), r_k.reshape(1, D),
      lnx_w.reshape(1, D), lnx_b.reshape(1, D))


def _res_a_kernel(x_ref, mix_ref, gp_ref, gkv_ref, gb_ref, h_ref, hkv_ref, hb_ref):
    h = x_ref[...] + _rms(mix_ref[...], gp_ref[...])
    h_ref[...] = h
    hkv_ref[...] = _rms(h, gkv_ref[...]).astype(hkv_ref.dtype)
    hb_ref[...] = _rms(h, gb_ref[...]).astype(hb_ref.dtype)


def _res_a(x, mix, g_post, g_kv, g_b):
    M, D = x.shape
    tm = _tile(M, 128, 8)
    blk = pl.BlockSpec((tm, D), lambda i: (i, 0))
    row = pl.BlockSpec((1, D), lambda i: (0, 0))
    return pl.pallas_call(
        _res_a_kernel,
        out_shape=[jax.ShapeDtypeStruct((M, D), F32), jax.ShapeDtypeStruct((M, D), BF16),
                   jax.ShapeDtypeStruct((M, D), BF16)],
        grid=(M // tm,),
        in_specs=[blk, blk, row, row, row],
        out_specs=[blk, blk, blk],
        compiler_params=_params(("parallel",)),
        name="residual_a",
    )(x, mix, g_post.reshape(1, D), g_kv.reshape(1, D), g_b.reshape(1, D))


def _res_b_kernel(h_ref, mix_ref, g_ref, o_ref):
    o_ref[...] = h_ref[...] + _rms(mix_ref[...], g_ref[...])


def _res_b(h, mix, g):
    M, D = h.shape
    tm = _tile(M, 256, 8)
    blk = pl.BlockSpec((tm, D), lambda i: (i, 0))
    return pl.pallas_call(
        _res_b_kernel,
        out_shape=jax.ShapeDtypeStruct((M, D), F32),
        grid=(M // tm,),
        in_specs=[blk, blk, pl.BlockSpec((1, D), lambda i: (0, 0))],
        out_specs=blk,
        compiler_params=_params(("parallel",)),
        name="residual_b",
    )(h, mix, g.reshape(1, D))


def _attn_kernel(q_ref, gate_ref, k_ref, v_ref, o_ref, kb_ref, vt_ref, km_ref, sel_ref,
                 sa_ref, sb_ref, pa_ref, pb_ref, top_ref, sum_ref, *, n_heads):
    BS = MOBA_BLOCK
    G = ATTN_GROUP
    assert G & (G - 1) == 0, "group size must be a power of two"
    T = k_ref.shape[1]
    NB = T // BS
    h = pl.program_id(1)
    i = pl.program_id(2)

    Dh = MOBA_HEAD_DIM
    assert BS >= Dh
    slope = jnp.exp2(-8.0 * (jnp.zeros((1, BS), F32) + (h + 1).astype(F32)) / n_heads)
    s_hi = slope.astype(BF16).astype(F32)
    s_lo = slope - s_hi

    @pl.when(i == 0)
    def _():
        kb_ref[:, :Dh] = k_ref[0].astype(BF16)
        lane = lax.broadcasted_iota(jnp.int32, (T, Dh), 1)
        key_in_block = (lax.broadcasted_iota(jnp.int32, (T, Dh), 0) % BS).astype(F32)
        extra = jnp.where(lane < 2, key_in_block,
                          jnp.where(lane == 2, s_hi[:, :Dh], jnp.where(lane == 3, s_lo[:, :Dh], 0.0)))
        kb_ref[:, Dh:] = extra.astype(BF16)
        for j in range(NB):
            rows = slice(j * BS, (j + 1) * BS)
            vt_ref[j] = v_ref[0, rows, :].T.astype(BF16)
            km_ref[j : j + 1, :] = jnp.mean(k_ref[0, rows, :], axis=0, keepdims=True)

    scale = MOBA_HEAD_DIM ** -0.5
    q_t = q_ref[0].T
    sub = lax.broadcasted_iota(jnp.int32, (Dh, BS), 0)
    neg_col = -lax.broadcasted_iota(jnp.int32, (Dh, BS), 1).astype(F32)
    q_extra = jnp.where(sub == 0, s_hi, jnp.where(sub == 1, s_lo, jnp.where(sub < 4, neg_col, 0.0)))
    q16 = jnp.concatenate([(q_t * scale).astype(BF16), q_extra.astype(BF16)], axis=0)

    km_hi, km_lo = _split(km_ref[...])
    qt_hi, qt_lo = _split(q_t)
    gate_s = _dot(jnp.concatenate([km_hi, km_hi, km_lo], axis=1),
                  jnp.concatenate([qt_hi, qt_lo, qt_hi], axis=0))
    blk = lax.broadcasted_iota(jnp.int32, gate_s.shape, 0)
    neg_inf = jnp.float32(-jnp.inf)
    gate_s = jnp.where(blk < i, gate_s, neg_inf)
    sel = jnp.zeros(gate_s.shape, F32)
    for _ in range(min(MOBA_TOPK, NB)):
        top = jnp.max(gate_s, axis=0, keepdims=True)
        first = jnp.min(jnp.where(gate_s == top, blk, NB), axis=0, keepdims=True)
        hit = blk == first
        sel = jnp.where(hit & (top > neg_inf), 1.0, sel)
        gate_s = jnp.where(hit, neg_inf, gate_s)
    sel_ref[...] = sel

    key_row = lax.broadcasted_iota(jnp.int32, (BS, BS), 0)
    query_col = lax.broadcasted_iota(jnp.int32, (BS, BS), 1)

    own = _dot(kb_ref[pl.ds(pl.multiple_of(i * BS, BS), BS), :], q16)
    s = jnp.where(query_col >= key_row, own, neg_inf)
    m = jnp.max(s, axis=0, keepdims=True)
    p = jnp.exp(s - m)
    l = jnp.sum(p, axis=0, keepdims=True)
    acc = _dot(vt_ref[i], p.astype(BF16))

    def issue_scores(k, s_ref):
        first = jnp.minimum(k * G, NB - G)
        s_ref[...] = _dot(kb_ref[pl.ds(pl.multiple_of(first * BS, BS), G * BS), :], q16)

    def normalise(k, s_ref, p_ref, stat_row):
        for u in range(G):
            jb = k * G + u
            in_past = (jb < i).astype(F32)
            keep = sel_ref[pl.ds(jnp.minimum(jb, NB - 1), 1), :] * in_past > 0.0
            shift = slope * ((jb - i) * BS).astype(F32)
            rows = slice(u * BS, (u + 1) * BS)
            t = s_ref[rows, :]
            c = jnp.max(t, axis=0, keepdims=True)
            p = jnp.exp(t - c)
            p_ref[rows, :] = p.astype(BF16)
            top_ref[stat_row + u:stat_row + u + 1, :] = jnp.where(keep, c + shift, neg_inf)
            sum_ref[stat_row + u:stat_row + u + 1, :] = jnp.sum(p, axis=0, keepdims=True)

    def values(k, p_ref):
        return [_dot(vt_ref[jnp.clip(k * G + u, 0, NB - 1)], p_ref[u * BS:(u + 1) * BS, :])
                for u in range(G)]

    def read_stats():
        return ([top_ref[n:n + 1, :] for n in range(2 * G)],
                [sum_ref[n:n + 1, :] for n in range(2 * G)])

    def merge(outs, stats, state):
        tops, sums = stats
        m, l, acc = state
        m_new = m
        for top in tops:
            m_new = jnp.maximum(m_new, top)
        alpha = jnp.exp(m - m_new)
        l = l * alpha
        acc = acc * alpha
        for u in range(len(outs)):
            beta = jnp.exp(tops[u] - m_new)
            l = l + sums[u] * beta
            acc = acc + outs[u] * beta
        return m_new, l, acc

    def two_groups(kk, state):
        k = 2 * kk
        stats = read_stats()
        outs = values(k - 2, pa_ref) + values(k - 1, pb_ref)
        normalise(k, sa_ref, pa_ref, 0)
        issue_scores(k + 2, sa_ref)
        normalise(k + 1, sb_ref, pb_ref, G)
        issue_scores(k + 3, sb_ref)
        return merge(outs, stats, state)

    issue_scores(0, sa_ref)
    issue_scores(1, sb_ref)
    pa_ref[...] = jnp.zeros_like(pa_ref)
    pb_ref[...] = jnp.zeros_like(pb_ref)
    top_ref[...] = jnp.full(top_ref.shape, neg_inf, F32)
    sum_ref[...] = jnp.zeros_like(sum_ref)
    n_passes = lax.shift_right_logical(i + (2 * G - 1), jnp.int32((2 * G).bit_length() - 1))
    state = lax.fori_loop(0, n_passes, two_groups, (m, l, acc))
    last = 2 * n_passes
    m, l, acc = merge(values(last - 2, pa_ref) + values(last - 1, pb_ref), read_stats(), state)

    att = (acc / l).T
    gate = gate_ref[0]
    o_ref[0] = (att * (gate / (1.0 + jnp.exp(-gate)))).astype(o_ref.dtype)


def _attention(qg, kv, n_heads):
    B, T, D2 = qg.shape
    D = D2 // 2
    BS, Dh = MOBA_BLOCK, MOBA_HEAD_DIM
    NB = T // BS
    return pl.pallas_call(
        functools.partial(_attn_kernel, n_heads=n_heads),
        out_shape=jax.ShapeDtypeStruct((B, T, D), BF16),
        grid=(B, n_heads, NB),
        in_specs=[
            pl.BlockSpec((1, BS, Dh), lambda b, h, i: (b, i, h)),
            pl.BlockSpec((1, BS, Dh), lambda b, h, i: (b, i, n_heads + h)),
            pl.BlockSpec((1, T, Dh), lambda b, h, i: (b, 0, h)),
            pl.BlockSpec((1, T, Dh), lambda b, h, i: (b, 0, n_heads + h)),
        ],
        out_specs=pl.BlockSpec((1, BS, Dh), lambda b, h, i: (b, i, h)),
        scratch_shapes=[
            pltpu.VMEM((T, 2 * Dh), BF16),
            pltpu.VMEM((NB, Dh, BS), BF16),
            pltpu.VMEM((NB, Dh), F32),
            pltpu.VMEM((NB, BS), F32),
            pltpu.VMEM((ATTN_GROUP * BS, BS), F32),
            pltpu.VMEM((ATTN_GROUP * BS, BS), F32),
            pltpu.VMEM((ATTN_GROUP * BS, BS), BF16),
            pltpu.VMEM((ATTN_GROUP * BS, BS), BF16),
            pltpu.VMEM((2 * ATTN_GROUP, BS), F32),
            pltpu.VMEM((2 * ATTN_GROUP, BS), F32),
        ],
        compiler_params=_params(("parallel", "parallel", "arbitrary")),
        name="moba_attention",
    )(qg, qg, kv, kv)


def kernel(x, a_pre_g, a_post_g, a_mu, a_w_in, a_w0, a_w1, a_w2, a_a0, a_a1, a_a2, a_k_k, a_k_a,
           a_r_k, a_lnx_w, a_lnx_b, a_w_o, kv_norm_g, w_k, w_v, b_pre_g, b_post_g, b_w_qg, b_w_o):
    B, T, D = x.shape
    M = B * T
    assert a_pre_g.shape[0] == 1 and b_pre_g.shape[0] == 1, "one RWKV layer then one MoBA layer"
    assert T % MOBA_BLOCK == 0 and D % LANES == 0 and T % SCAN_CHUNK == 0
    n_moba_heads = D // MOBA_HEAD_DIM

    mixes = _prep(x, a_pre_g[0], a_mu[0]).reshape(a_mu.shape[1], M, D)
    rkvg = _matmul(mixes, a_w_in[0].astype(BF16), F32, name="rwkv_proj")
    w_lora = jnp.stack([a_w1[0], a_a1[0]]).astype(BF16)
    lo = _matmul(mixes, w_lora, F32, x_off=4, name="rwkv_lora")
    lw, ai = _rates(lo, a_w0[0], a_w2[0].astype(BF16), a_a0[0], a_a2[0].astype(BF16))
    yg = _scan(rkvg.reshape(4, B, T, D), lw.reshape(B, T, D), ai.reshape(B, T, D), a_k_k[0],
               a_k_a[0], a_r_k[0], a_lnx_w[0], a_lnx_b[0])
    mix_a = _matmul(yg.reshape(1, M, D), a_w_o[0].astype(BF16)[None], F32, name="rwkv_out")[0]
    h1, hkv, hb = _res_a(x.reshape(M, D), mix_a, a_post_g[0], kv_norm_g, b_pre_g[0])

    w_kv = jnp.concatenate([w_k, w_v], axis=1).astype(BF16)[None]
    kv = _matmul(hkv[None], w_kv, F32, name="kv_proj")[0]
    qg = _matmul(hb[None], b_w_qg[0].astype(BF16)[None], F32, name="qg_proj")[0]
    yb = _attention(qg.reshape(B, T, 2 * D), kv.reshape(B, T, 2 * D), n_moba_heads)
    mix_b = _matmul(yb.reshape(1, M, D), b_w_o[0].astype(BF16)[None], F32, name="moba_out")[0]
    return _res_b(h1, mix_b, b_post_g[0]).reshape(B, T, D)
```

```python
import functools

import jax
import jax.numpy as jnp
from jax import lax
from jax.experimental import pallas as pl
from jax.experimental.pallas import tpu as pltpu

RWKV_HEAD_DIM = 64
MOBA_HEAD_DIM = 128
MOBA_BLOCK = 256
MOBA_TOPK = 3
GN_EPS = 64e-5
NORM_EPS = 1e-6

LANES = 128
MXU_WIDTH = 256
SCAN_CHUNK = RWKV_HEAD_DIM
PACK_LANES = MXU_WIDTH
HEADS_PER_PACK = PACK_LANES // RWKV_HEAD_DIM
SCAN_LANES = 2048
ONES_ROWS = 16
LOG2_E = 1.4426950408889634
ATTN_GROUP = 2
VMEM_LIMIT = 56 * 1024 * 1024

F32 = jnp.float32
BF16 = jnp.bfloat16


def _tile(n, pref, align):
    if n <= pref:
        return n
    t = (pref // align) * align
    while t >= align:
        if n % t == 0:
            return t
        t -= align
    return n


def _params(sem):
    return pltpu.CompilerParams(dimension_semantics=sem, vmem_limit_bytes=VMEM_LIMIT)


def _rms(x, g):
    return x * lax.rsqrt(jnp.mean(x * x, axis=-1, keepdims=True) + NORM_EPS) * g


def _dot(a, b):
    return jnp.dot(a, b, preferred_element_type=F32)


def _dot_nt(a, b):
    return lax.dot_general(a, b, (((1,), (1,)), ((), ())), preferred_element_type=F32)


def _dot_tn(a, b):
    return lax.dot_general(a, b, (((0,), (0,)), ((), ())), preferred_element_type=F32)


def _split(x):
    hi = x.astype(BF16)
    lo = (x - hi.astype(F32)).astype(BF16)
    return hi, lo


def _prep_kernel(x_ref, xp_ref, g_ref, mu_ref, o_ref):
    i = pl.program_id(1)
    g = g_ref[...]
    hn = _rms(x_ref[0], g)
    prev = _rms(xp_ref[0, 7:8, :], g)
    prev = jnp.where(i == 0, 0.0, prev)
    row = lax.broadcasted_iota(jnp.int32, hn.shape, 0)
    shifted = jnp.where(row == 0, prev, pltpu.roll(hn, shift=1, axis=0))
    dx = shifted - hn
    for n in range(o_ref.shape[0]):
        o_ref[n, 0] = (hn + dx * mu_ref[n : n + 1, :]).astype(o_ref.dtype)


def _prep(x, g, mu):
    B, T, D = x.shape
    n_mix = mu.shape[0]
    tt = _tile(T, 128, 8)
    return pl.pallas_call(
        _prep_kernel,
        out_shape=jax.ShapeDtypeStruct((n_mix, B, T, D), BF16),
        grid=(B, T // tt),
        in_specs=[
            pl.BlockSpec((1, tt, D), lambda b, i: (b, i, 0)),
            pl.BlockSpec((1, 8, D), lambda b, i: (b, jnp.maximum(i * (tt // 8) - 1, 0), 0)),
            pl.BlockSpec((1, D), lambda b, i: (0, 0)),
            pl.BlockSpec((n_mix, D), lambda b, i: (0, 0)),
        ],
        out_specs=pl.BlockSpec((n_mix, 1, tt, D), lambda b, i: (0, b, i, 0)),
        compiler_params=_params(("parallel", "parallel")),
        name="rwkv_prep",
    )(x, x, g.reshape(1, D), mu)


def _mm_kernel(x_ref, w_ref, o_ref):
    o_ref[0] = _dot(x_ref[0], w_ref[0]).astype(o_ref.dtype)


def _matmul(x, w, out_dtype, *, x_off=0, name="matmul"):
    G, K, N = w.shape
    M = x.shape[1]
    tm = _tile(M, 1024, 8)
    tn = _tile(N, 512, LANES)
    return pl.pallas_call(
        _mm_kernel,
        out_shape=jax.ShapeDtypeStruct((G, M, N), out_dtype),
        grid=(G, M // tm, N // tn),
        in_specs=[
            pl.BlockSpec((1, tm, K), lambda g, i, j: (g + x_off, i, 0)),
            pl.BlockSpec((1, K, tn), lambda g, i, j: (g, 0, j)),
        ],
        out_specs=pl.BlockSpec((1, tm, tn), lambda g, i, j: (g, i, j)),
        compiler_params=_params(("parallel", "parallel", "arbitrary")),
        name=name,
    )(x, w)


def _rates_kernel(lo_ref, w0_ref, w2_ref, a0_ref, a2_ref, lw_ref, ai_ref):
    z = w0_ref[...] + _dot(jnp.tanh(lo_ref[0]).astype(BF16), w2_ref[...])
    u = -z
    softplus = jnp.maximum(u, 0.0) + jnp.log1p(jnp.exp(-jnp.abs(u)))
    wlog = -softplus - 0.5
    lw_ref[...] = -jnp.exp(wlog)
    y = a0_ref[...] + _dot(lo_ref[1].astype(BF16), a2_ref[...])
    ai_ref[...] = 1.0 / (1.0 + jnp.exp(-y))


def _rates(lo, w0, w2, a0, a2):
    _, M, R = lo.shape
    D = w2.shape[1]
    tm = _tile(M, 256, 8)
    row = pl.BlockSpec((1, D), lambda i: (0, 0))
    mat = pl.BlockSpec((R, D), lambda i: (0, 0))
    out = pl.BlockSpec((tm, D), lambda i: (i, 0))
    return pl.pallas_call(
        _rates_kernel,
        out_shape=[jax.ShapeDtypeStruct((M, D), F32)] * 2,
        grid=(M // tm,),
        in_specs=[pl.BlockSpec((2, tm, R), lambda i: (0, i, 0)), row, mat, row, mat],
        out_specs=[out, out],
        compiler_params=_params(("parallel",)),
        name="rwkv_rates",
    )(lo, w0.reshape(1, D), w2, a0.reshape(1, D), a2)


def _scan_kernel(r_ref, k_ref, v_ref, g_ref, lw_ref, ai_ref, kk_ref, ka_ref, rk_ref,
                 lnw_ref, lnb_ref, o_ref, s_ref):
    C = SCAN_CHUNK
    N = RWKV_HEAD_DIM
    L = PACK_LANES
    HP = HEADS_PER_PACK
    packs = range(o_ref.shape[2] // L)
    lanes = [slice(p * L, (p + 1) * L) for p in packs]

    @pl.when(pl.program_id(2) == 0)
    def _():
        s_ref[...] = jnp.zeros_like(s_ref)

    t_row = lax.broadcasted_iota(jnp.int32, (C, HP * C), 0)
    s_col = lax.broadcasted_iota(jnp.int32, (C, HP * C), 1) % C
    strict = s_col < t_row
    incl = s_col <= t_row
    eye_packed = (s_col == t_row).astype(F32)
    tri = (lax.broadcasted_iota(jnp.int32, (C, C), 1)
           <= lax.broadcasted_iota(jnp.int32, (C, C), 0)).astype(BF16)
    same_head = ((lax.broadcasted_iota(jnp.int32, (L, L), 0) // N)
                 == (lax.broadcasted_iota(jnp.int32, (L, L), 1) // N))
    ones_bd = same_head.astype(BF16)
    lane_head = lax.broadcasted_iota(jnp.int32, (C, L), 1) // N

    def bd(x):
        return jnp.concatenate(
            [jnp.where(lane_head == h, x, jnp.zeros_like(x)) for h in range(HP)], axis=0)

    def seg_sums(xs):
        parts = []
        for x in xs:
            parts.extend(_split(x))
        res = _dot(jnp.concatenate(parts, axis=0), ones_bd)
        return [res[2 * n * C:(2 * n + 1) * C] + res[(2 * n + 1) * C:(2 * n + 2) * C]
                for n in range(len(xs))]

    r = [r_ref[0, 0, :, s].astype(F32) for s in lanes]
    k = [k_ref[0, 0, :, s].astype(F32) for s in lanes]
    v = [v_ref[0, 0, :, s].astype(F32) for s in lanes]
    lw = [lw_ref[0, :, s] for s in lanes]
    ai = [ai_ref[0, :, s] for s in lanes]

    kk = [k[p] * kk_ref[:, lanes[p]] for p in packs]
    ss = [seg_sums([kk[p] * kk[p]])[0] for p in packs]
    cum = []
    for p in packs:
        hi, lo = _split(lw[p])
        both = _dot(tri, jnp.concatenate([hi, lo], axis=1))
        cum.append(both[:, :L] + both[:, L:])

    k2, at, rt, bh, kh, g_last, v16, lhs, rhs = [], [], [], [], [], [], [], [], []
    for p in packs:
        kkn = kk[p] / jnp.maximum(jnp.sqrt(ss[p]), 1e-12)
        k2.append(k[p] * (1.0 + (ai[p] - 1.0) * ka_ref[:, lanes[p]]))
        b = kkn * ai[p]
        cum_last = cum[p][C - 1:C, :]
        g_inv = jnp.exp(-cum[p])
        g_tail = jnp.exp(cum_last - cum[p])
        at.append((-kkn * jnp.exp(cum[p] - lw[p])).astype(BF16))
        rt.append(r[p] * jnp.exp(cum[p]))
        bh.append((b * g_tail).astype(BF16))
        kh.append((k2[p] * g_tail).astype(BF16))
        g_last.append(jnp.exp(cum_last))
        v16.append(v[p].astype(BF16))
        lhs.append(jnp.concatenate([at[p], rt[p].astype(BF16)], axis=0))
        rhs.append(jnp.concatenate([bd((b * g_inv).astype(BF16)),
                                    bd((k2[p] * g_inv).astype(BF16))], axis=0))

    sbk = [_dot_nt(lhs[p], rhs[p]) for p in packs]
    a_ab = [jnp.where(strict, sbk[p][:C, :HP * C], 0.0) for p in packs]
    a_ak = [jnp.where(strict, sbk[p][:C, HP * C:], 0.0).astype(BF16) for p in packs]
    a_r = [jnp.concatenate([jnp.where(incl, sbk[p][C:, :HP * C], 0.0),
                            jnp.where(incl, sbk[p][C:, HP * C:], 0.0)], axis=1).astype(BF16)
           for p in packs]
    bdv = [bd(v16[p]) for p in packs]
    x1 = [_dot(a_ak[p], bdv[p]) for p in packs]

    mpow = a_ab
    tinv = [eye_packed + a_ab[p] for p in packs]
    n_sq = 1
    while n_sq * 2 < C:
        m16 = [mpow[p].astype(BF16) for p in packs]
        mpow = [_dot(m16[p], bd(m16[p])) for p in packs]
        step = [_dot(tinv[p].astype(BF16), bd(mpow[p].astype(BF16))) for p in packs]
        tinv = [tinv[p] + step[p] for p in packs]
        n_sq *= 2

    ua = [_dot(tinv[p].astype(BF16),
               jnp.concatenate([bd(x1[p].astype(BF16)), bd(at[p])], axis=1)) for p in packs]
    u0 = [ua[p][:, :L].astype(BF16) for p in packs]
    atp = [ua[p][:, L:].astype(BF16) for p in packs]
    zeros = jnp.zeros((HP * C, L), BF16)
    ry = [_dot(a_r[p], jnp.concatenate(
        [jnp.concatenate([bd(atp[p]), bd(u0[p])], axis=1),
         jnp.concatenate([zeros, bdv[p]], axis=1)], axis=0)) for p in packs]
    rp = [(rt[p] + ry[p][:, :L]).astype(BF16) for p in packs]
    y0 = [ry[p][:, L:] for p in packs]

    zc = jnp.zeros((C, L), BF16)
    pq = [_dot_tn(jnp.concatenate([jnp.concatenate([atp[p], u0[p]], axis=1),
                                   jnp.concatenate([zc, v16[p]], axis=1)], axis=0),
                  jnp.concatenate([bh[p], kh[p]], axis=0)) for p in packs]
    pt = [jnp.where(same_head, pq[p][:L], 0.0).astype(BF16) for p in packs]
    qt = [jnp.where(same_head, pq[p][L:], 0.0) for p in packs]

    s0 = [s_ref[p] for p in packs]
    s16 = [s0[p].astype(BF16) for p in packs]
    y = [_dot_nt(rp[p], s16[p]) + y0[p] for p in packs]
    for p in packs:
        s_ref[p] = s0[p] * g_last[p] + _dot(s16[p], pt[p]) + qt[p]

    st = [seg_sums([y[p], r[p] * k2[p] * rk_ref[:, lanes[p]]]) for p in packs]
    yc = [y[p] - st[p][0] * (1.0 / N) for p in packs]
    var = [seg_sums([yc[p] * yc[p]])[0] * (1.0 / N) for p in packs]
    for p in packs:
        yn = yc[p] * lax.rsqrt(var[p] + GN_EPS) * lnw_ref[:, lanes[p]] + lnb_ref[:, lanes[p]]
        gate = g_ref[0, 0, :, lanes[p]].astype(F32)
        out = (yn + st[p][1] * v[p]) * (gate / (1.0 + jnp.exp(-gate)))
        o_ref[0, :, lanes[p]] = out.astype(o_ref.dtype)


def _scan(rkvg, lw, ai, k_k, k_a, r_k, lnx_w, lnx_b):
    _, B, T, D = rkvg.shape
    C = SCAN_CHUNK
    W = _tile(D, SCAN_LANES, PACK_LANES)
    act = lambda n: pl.BlockSpec((1, 1, C, W), lambda b, g, c, n=n: (n, b, c, g))
    seq = pl.BlockSpec((1, C, W), lambda b, g, c: (b, c, g))
    row = pl.BlockSpec((1, W), lambda b, g, c: (0, g))
    return pl.pallas_call(
        _scan_kernel,
        out_shape=jax.ShapeDtypeStruct((B, T, D), BF16),
        grid=(B, D // W, T // C),
        in_specs=[act(0), act(1), act(2), act(3), seq, seq, row, row, row, row, row],
        out_specs=seq,
        scratch_shapes=[pltpu.VMEM((W // PACK_LANES, PACK_LANES, PACK_LANES), F32)],
        compiler_params=_params(("parallel", "parallel", "arbitrary")),
        name="rwkv_scan",
    )(rkvg, rkvg, rkvg, rkvg, lw, ai, k_k.reshape(1, D), k_a.reshape(1, D), r_k.reshape(1, D),
      lnx_w.reshape(1, D), lnx_b.reshape(1, D))


def _res_a_kernel(x_ref, mix_ref, gp_ref, gkv_ref, gb_ref, h_ref, hkv_ref, hb_ref):
    h = x_ref[...] + _rms(mix_ref[...], gp_ref[...])
    h_ref[...] = h
    hkv_ref[...] = _rms(h, gkv_ref[...]).astype(hkv_ref.dtype)
    hb_ref[...] = _rms(h, gb_ref[...]).astype(hb_ref.dtype)


def _res_a(x, mix, g_post, g_kv, g_b):
    M, D = x.shape
    tm = _tile(M, 128, 8)
    blk = pl.BlockSpec((tm, D), lambda i: (i, 0))
    row = pl.BlockSpec((1, D), lambda i: (0, 0))
    return pl.pallas_call(
        _res_a_kernel,
        out_shape=[jax.ShapeDtypeStruct((M, D), F32), jax.ShapeDtypeStruct((M, D), BF16),
                   jax.ShapeDtypeStruct((M, D), BF16)],
        grid=(M // tm,),
        in_specs=[blk, blk, row, row, row],
        out_specs=[blk, blk, blk],
        compiler_params=_params(("parallel",)),
        name="residual_a",
    )(x, mix, g_post.reshape(1, D), g_kv.reshape(1, D), g_b.reshape(1, D))


def _res_b_kernel(h_ref, mix_ref, g_ref, o_ref):
    o_ref[...] = h_ref[...] + _rms(mix_ref[...], g_ref[...])


def _res_b(h, mix, g):
    M, D = h.shape
    tm = _tile(M, 256, 8)
    blk = pl.BlockSpec((tm, D), lambda i: (i, 0))
    return pl.pallas_call(
        _res_b_kernel,
        out_shape=jax.ShapeDtypeStruct((M, D), F32),
        grid=(M // tm,),
        in_specs=[blk, blk, pl.BlockSpec((1, D), lambda i: (0, 0))],
        out_specs=blk,
        compiler_params=_params(("parallel",)),
        name="residual_b",
    )(h, mix, g.reshape(1, D))


def _attn_kernel(q_ref, gate_ref, k_ref, v_ref, o_ref, kb_ref, vt_ref, km_ref, sel_ref,
                 sa_ref, sb_ref, pa_ref, pb_ref, po_ref, top_ref, *, n_heads):
    BS = MOBA_BLOCK
    G = ATTN_GROUP
    assert G & (G - 1) == 0, "group size must be a power of two"
    T = k_ref.shape[1]
    NB = T // BS
    h = pl.program_id(1)
    i = pl.program_id(2)

    Dh = MOBA_HEAD_DIM
    assert BS >= Dh
    slope = LOG2_E * jnp.exp2(
        -8.0 * (jnp.zeros((1, BS), F32) + (h + 1).astype(F32)) / n_heads)
    s_hi = slope.astype(BF16).astype(F32)
    s_lo = slope - s_hi

    @pl.when(i == 0)
    def _():
        kb_ref[:, :Dh] = k_ref[0].astype(BF16)
        lane = lax.broadcasted_iota(jnp.int32, (T, Dh), 1)
        key_in_block = (lax.broadcasted_iota(jnp.int32, (T, Dh), 0) % BS).astype(F32)
        extra = jnp.where(lane < 2, key_in_block,
                          jnp.where(lane == 2, s_hi[:, :Dh], jnp.where(lane == 3, s_lo[:, :Dh], 0.0)))
        kb_ref[:, Dh:] = extra.astype(BF16)
        pa_ref[...] = jnp.zeros_like(pa_ref)
        pb_ref[...] = jnp.zeros_like(pb_ref)
        for j in range(NB):
            rows = slice(j * BS, (j + 1) * BS)
            vt_ref[j, :Dh] = v_ref[0, rows, :].astype(F32).T.astype(BF16)
            vt_ref[j, Dh:] = jnp.ones((ONES_ROWS, BS), BF16)
            km_ref[j : j + 1, :] = jnp.mean(k_ref[0, rows, :].astype(F32), axis=0, keepdims=True)

    scale = LOG2_E * MOBA_HEAD_DIM ** -0.5
    q_t = q_ref[0].T
    sub = lax.broadcasted_iota(jnp.int32, (Dh, BS), 0)
    neg_col = -lax.broadcasted_iota(jnp.int32, (Dh, BS), 1).astype(F32)
    q_extra = jnp.where(sub == 0, s_hi, jnp.where(sub == 1, s_lo, jnp.where(sub < 4, neg_col, 0.0)))
    q16 = jnp.concatenate([(q_t * scale).astype(BF16), q_extra.astype(BF16)], axis=0)

    km_hi, km_lo = _split(km_ref[...])
    qt_hi, qt_lo = _split(q_t)
    gate_s = _dot(jnp.concatenate([km_hi, km_hi, km_lo], axis=1),
                  jnp.concatenate([qt_hi, qt_lo, qt_hi], axis=0))
    blk = lax.broadcasted_iota(jnp.int32, gate_s.shape, 0)
    neg_inf = jnp.float32(-jnp.inf)
    gate_s = jnp.where(blk < i, gate_s, neg_inf)
    sel = jnp.zeros(gate_s.shape, F32)
    for _ in range(min(MOBA_TOPK, NB)):
        top = jnp.max(gate_s, axis=0, keepdims=True)
        first = jnp.min(jnp.where(gate_s == top, blk, NB), axis=0, keepdims=True)
        hit = blk == first
        sel = jnp.where(hit & (top > neg_inf), 1.0, sel)
        gate_s = jnp.where(hit, neg_inf, gate_s)
    sel_ref[...] = sel

    key_row = lax.broadcasted_iota(jnp.int32, (BS, BS), 0)
    query_col = lax.broadcasted_iota(jnp.int32, (BS, BS), 1)

    own = _dot(kb_ref[pl.ds(pl.multiple_of(i * BS, BS), BS), :], q16)
    s = jnp.where(query_col >= key_row, own, neg_inf)
    top_own = jnp.max(s, axis=0, keepdims=True)
    po_ref[...] = jnp.exp2(s - top_own).astype(BF16)
    acc0 = jnp.zeros((Dh + ONES_ROWS, BS), F32)

    def issue_scores(k, s_ref):
        first = jnp.minimum(k * G, NB - G)
        s_ref[...] = _dot(kb_ref[pl.ds(pl.multiple_of(first * BS, BS), G * BS), :], q16)

    def normalise(k, s_ref, p_ref, stat_row):
        for u in range(G):
            jb = k * G + u
            in_past = (jb < i).astype(F32)
            keep = sel_ref[pl.ds(jnp.minimum(jb, NB - 1), 1), :] * in_past > 0.0
            shift = slope * ((jb - i) * BS).astype(F32)
            rows = slice(u * BS, (u + 1) * BS)
            t = s_ref[rows, :]
            c = jnp.max(t, axis=0, keepdims=True)
            p_ref[rows, :] = jnp.exp2(t - c).astype(BF16)
            top_ref[stat_row + u:stat_row + u + 1, :] = jnp.where(keep, c + shift, neg_inf)

    def values(k, p_ref):
        return [_dot(vt_ref[jnp.clip(k * G + u, 0, NB - 1)], p_ref[u * BS:(u + 1) * BS, :])
                for u in range(G)]

    def read_stats():
        return [top_ref[n:n + 1, :] for n in range(2 * G)]

    def merge(outs, tops, state):
        m, acc = state
        m_new = m
        for top in tops:
            m_new = jnp.maximum(m_new, top)
        acc = acc * jnp.exp2(m - m_new)
        for u in range(len(outs)):
            acc = acc + outs[u] * jnp.exp2(tops[u] - m_new)
        return m_new, acc

    def two_groups(kk, state):
        k = 2 * kk
        stats = read_stats()
        outs = values(k - 2, pa_ref) + values(k - 1, pb_ref)
        normalise(k, sa_ref, pa_ref, 0)
        issue_scores(k + 2, sa_ref)
        normalise(k + 1, sb_ref, pb_ref, G)
        issue_scores(k + 3, sb_ref)
        return merge(outs, stats, state)

    issue_scores(0, sa_ref)
    issue_scores(1, sb_ref)
    top_ref[...] = jnp.full(top_ref.shape, neg_inf, F32)
    n_passes = lax.shift_right_logical(i + (2 * G - 1), jnp.int32((2 * G).bit_length() - 1))
    state = lax.fori_loop(0, n_passes, two_groups, (top_own, acc0))
    last = 2 * n_passes
    outs = (values(last - 2, pa_ref) + values(last - 1, pb_ref)
            + [_dot(vt_ref[i], po_ref[...])])
    m, acc = merge(outs, read_stats() + [top_own], state)

    att = (acc[:Dh] / acc[Dh:Dh + 1]).T
    gate = gate_ref[0]
    o_ref[0] = (att * (gate / (1.0 + jnp.exp(-gate)))).astype(o_ref.dtype)


def _attention(qg, kv, n_heads):
    B, T, D2 = qg.shape
    D = D2 // 2
    BS, Dh = MOBA_BLOCK, MOBA_HEAD_DIM
    NB = T // BS
    return pl.pallas_call(
        functools.partial(_attn_kernel, n_heads=n_heads),
        out_shape=jax.ShapeDtypeStruct((B, T, D), BF16),
        grid=(B, n_heads, NB),
        in_specs=[
            pl.BlockSpec((1, BS, Dh), lambda b, h, i: (b, i, h)),
            pl.BlockSpec((1, BS, Dh), lambda b, h, i: (b, i, n_heads + h)),
            pl.BlockSpec((1, T, Dh), lambda b, h, i: (b, 0, h)),
            pl.BlockSpec((1, T, Dh), lambda b, h, i: (b, 0, n_heads + h)),
        ],
        out_specs=pl.BlockSpec((1, BS, Dh), lambda b, h, i: (b, i, h)),
        scratch_shapes=[
            pltpu.VMEM((T, 2 * Dh), BF16),
            pltpu.VMEM((NB, Dh + ONES_ROWS, BS), BF16),
            pltpu.VMEM((NB, Dh), F32),
            pltpu.VMEM((NB, BS), F32),
            pltpu.VMEM((ATTN_GROUP * BS, BS), F32),
            pltpu.VMEM((ATTN_GROUP * BS, BS), F32),
            pltpu.VMEM((ATTN_GROUP * BS, BS), BF16),
            pltpu.VMEM((ATTN_GROUP * BS, BS), BF16),
            pltpu.VMEM((BS, BS), BF16),
            pltpu.VMEM((2 * ATTN_GROUP, BS), F32),
        ],
        compiler_params=_params(("parallel", "parallel", "arbitrary")),
        name="moba_attention",
    )(qg, qg, kv, kv)


def kernel(x, a_pre_g, a_post_g, a_mu, a_w_in, a_w0, a_w1, a_w2, a_a0, a_a1, a_a2, a_k_k, a_k_a,
           a_r_k, a_lnx_w, a_lnx_b, a_w_o, kv_norm_g, w_k, w_v, b_pre_g, b_post_g, b_w_qg, b_w_o):
    B, T, D = x.shape
    M = B * T
    assert a_pre_g.shape[0] == 1 and b_pre_g.shape[0] == 1, "one RWKV layer then one MoBA layer"
    assert T % MOBA_BLOCK == 0 and D % LANES == 0 and T % SCAN_CHUNK == 0
    n_moba_heads = D // MOBA_HEAD_DIM

    mixes = _prep(x, a_pre_g[0], a_mu[0]).reshape(a_mu.shape[1], M, D)
    rkvg = _matmul(mixes, a_w_in[0].astype(BF16), BF16, name="rwkv_proj")
    w_lora = jnp.stack([a_w1[0], a_a1[0]]).astype(BF16)
    lo = _matmul(mixes, w_lora, F32, x_off=4, name="rwkv_lora")
    lw, ai = _rates(lo, a_w0[0], a_w2[0].astype(BF16), a_a0[0], a_a2[0].astype(BF16))
    yg = _scan(rkvg.reshape(4, B, T, D), lw.reshape(B, T, D), ai.reshape(B, T, D), a_k_k[0],
               a_k_a[0], a_r_k[0], a_lnx_w[0], a_lnx_b[0])
    mix_a = _matmul(yg.reshape(1, M, D), a_w_o[0].astype(BF16)[None], F32, name="rwkv_out")[0]
    h1, hkv, hb = _res_a(x.reshape(M, D), mix_a, a_post_g[0], kv_norm_g, b_pre_g[0])

    w_kv = jnp.concatenate([w_k, w_v], axis=1).astype(BF16)[None]
    kv = _matmul(hkv[None], w_kv, BF16, name="kv_proj")[0]
    qg = _matmul(hb[None], b_w_qg[0].astype(BF16)[None], F32, name="qg_proj")[0]
    yb = _attention(qg.reshape(B, T, 2 * D), kv.reshape(B, T, 2 * D), n_moba_heads)
    mix_b = _matmul(yb.reshape(1, M, D), b_w_o[0].astype(BF16)[None], F32, name="moba_out")[0]
    return _res_b(h1, mix_b, b_post_g[0]).reshape(B, T, D)
```

```python
import functools

import jax
import jax.numpy as jnp
from jax import lax
from jax.experimental import pallas as pl
from jax.experimental.pallas import tpu as pltpu

RWKV_HEAD_DIM = 64
MOBA_HEAD_DIM = 128
MOBA_BLOCK = 256
MOBA_TOPK = 3
GN_EPS = 64e-5
NORM_EPS = 1e-6

LANES = 128
MXU_WIDTH = 256
SCAN_CHUNK = RWKV_HEAD_DIM
PACK_LANES = MXU_WIDTH
HEADS_PER_PACK = PACK_LANES // RWKV_HEAD_DIM
SCAN_LANES = 2048
ONES_ROWS = 16
LOG2_E = 1.4426950408889634
ATTN_QUERY_BLOCKS = 2
ATTN_GROUP = 2
VMEM_LIMIT = 56 * 1024 * 1024

F32 = jnp.float32
BF16 = jnp.bfloat16


def _tile(n, pref, align):
    if n <= pref:
        return n
    t = (pref // align) * align
    while t >= align:
        if n % t == 0:
            return t
        t -= align
    return n


def _params(sem):
    return pltpu.CompilerParams(dimension_semantics=sem, vmem_limit_bytes=VMEM_LIMIT)


def _rms(x, g):
    return x * lax.rsqrt(jnp.mean(x * x, axis=-1, keepdims=True) + NORM_EPS) * g


def _dot(a, b):
    return jnp.dot(a, b, preferred_element_type=F32)


def _dot_nt(a, b):
    return lax.dot_general(a, b, (((1,), (1,)), ((), ())), preferred_element_type=F32)


def _dot_tn(a, b):
    return lax.dot_general(a, b, (((0,), (0,)), ((), ())), preferred_element_type=F32)


def _split(x):
    hi = x.astype(BF16)
    lo = (x - hi.astype(F32)).astype(BF16)
    return hi, lo


def _prep_kernel(x_ref, xp_ref, g_ref, mu_ref, o_ref):
    i = pl.program_id(1)
    g = g_ref[...]
    hn = _rms(x_ref[0], g)
    prev = _rms(xp_ref[0, 7:8, :], g)
    prev = jnp.where(i == 0, 0.0, prev)
    row = lax.broadcasted_iota(jnp.int32, hn.shape, 0)
    shifted = jnp.where(row == 0, prev, pltpu.roll(hn, shift=1, axis=0))
    dx = shifted - hn
    for n in range(o_ref.shape[0]):
        o_ref[n, 0] = (hn + dx * mu_ref[n : n + 1, :]).astype(o_ref.dtype)


def _prep(x, g, mu):
    B, T, D = x.shape
    n_mix = mu.shape[0]
    tt = _tile(T, 128, 8)
    return pl.pallas_call(
        _prep_kernel,
        out_shape=jax.ShapeDtypeStruct((n_mix, B, T, D), BF16),
        grid=(B, T // tt),
        in_specs=[
            pl.BlockSpec((1, tt, D), lambda b, i: (b, i, 0)),
            pl.BlockSpec((1, 8, D), lambda b, i: (b, jnp.maximum(i * (tt // 8) - 1, 0), 0)),
            pl.BlockSpec((1, D), lambda b, i: (0, 0)),
            pl.BlockSpec((n_mix, D), lambda b, i: (0, 0)),
        ],
        out_specs=pl.BlockSpec((n_mix, 1, tt, D), lambda b, i: (0, b, i, 0)),
        compiler_params=_params(("parallel", "parallel")),
        name="rwkv_prep",
    )(x, x, g.reshape(1, D), mu)


def _mm_kernel(x_ref, w_ref, o_ref):
    o_ref[0] = _dot(x_ref[0], w_ref[0]).astype(o_ref.dtype)


def _matmul(x, w, out_dtype, *, x_off=0, name="matmul"):
    G, K, N = w.shape
    M = x.shape[1]
    tm = _tile(M, 1024, 8)
    tn = _tile(N, 512, LANES)
    return pl.pallas_call(
        _mm_kernel,
        out_shape=jax.ShapeDtypeStruct((G, M, N), out_dtype),
        grid=(G, M // tm, N // tn),
        in_specs=[
            pl.BlockSpec((1, tm, K), lambda g, i, j: (g + x_off, i, 0)),
            pl.BlockSpec((1, K, tn), lambda g, i, j: (g, 0, j)),
        ],
        out_specs=pl.BlockSpec((1, tm, tn), lambda g, i, j: (g, i, j)),
        compiler_params=_params(("parallel", "parallel", "arbitrary")),
        name=name,
    )(x, w)


def _rates_kernel(lo_ref, w0_ref, w2_ref, a0_ref, a2_ref, lw_ref, ai_ref):
    z = w0_ref[...] + _dot(jnp.tanh(lo_ref[0]).astype(BF16), w2_ref[...])
    u = -z
    softplus = jnp.maximum(u, 0.0) + jnp.log1p(jnp.exp(-jnp.abs(u)))
    wlog = -softplus - 0.5
    lw_ref[...] = -jnp.exp(wlog)
    y = a0_ref[...] + _dot(lo_ref[1].astype(BF16), a2_ref[...])
    ai_ref[...] = 1.0 / (1.0 + jnp.exp(-y))


def _rates(lo, w0, w2, a0, a2):
    _, M, R = lo.shape
    D = w2.shape[1]
    tm = _tile(M, 256, 8)
    row = pl.BlockSpec((1, D), lambda i: (0, 0))
    mat = pl.BlockSpec((R, D), lambda i: (0, 0))
    out = pl.BlockSpec((tm, D), lambda i: (i, 0))
    return pl.pallas_call(
        _rates_kernel,
        out_shape=[jax.ShapeDtypeStruct((M, D), F32)] * 2,
        grid=(M // tm,),
        in_specs=[pl.BlockSpec((2, tm, R), lambda i: (0, i, 0)), row, mat, row, mat],
        out_specs=[out, out],
        compiler_params=_params(("parallel",)),
        name="rwkv_rates",
    )(lo, w0.reshape(1, D), w2, a0.reshape(1, D), a2)


def _scan_kernel(r_ref, k_ref, v_ref, g_ref, lw_ref, ai_ref, kk_ref, ka_ref, rk_ref,
                 lnw_ref, lnb_ref, o_ref, s_ref):
    C = SCAN_CHUNK
    N = RWKV_HEAD_DIM
    L = PACK_LANES
    HP = HEADS_PER_PACK
    packs = range(o_ref.shape[2] // L)
    lanes = [slice(p * L, (p + 1) * L) for p in packs]

    @pl.when(pl.program_id(2) == 0)
    def _():
        s_ref[...] = jnp.zeros_like(s_ref)

    t_row = lax.broadcasted_iota(jnp.int32, (C, HP * C), 0)
    s_col = lax.broadcasted_iota(jnp.int32, (C, HP * C), 1) % C
    strict = s_col < t_row
    incl = s_col <= t_row
    eye_packed = (s_col == t_row).astype(F32)
    tri = (lax.broadcasted_iota(jnp.int32, (C, C), 1)
           <= lax.broadcasted_iota(jnp.int32, (C, C), 0)).astype(BF16)
    same_head = ((lax.broadcasted_iota(jnp.int32, (L, L), 0) // N)
                 == (lax.broadcasted_iota(jnp.int32, (L, L), 1) // N))
    ones_bd = same_head.astype(BF16)
    lane_head = lax.broadcasted_iota(jnp.int32, (C, L), 1) // N

    def bd(x):
        return jnp.concatenate(
            [jnp.where(lane_head == h, x, jnp.zeros_like(x)) for h in range(HP)], axis=0)

    def seg_sums(xs):
        parts = []
        for x in xs:
            parts.extend(_split(x))
        res = _dot(jnp.concatenate(parts, axis=0), ones_bd)
        return [res[2 * n * C:(2 * n + 1) * C] + res[(2 * n + 1) * C:(2 * n + 2) * C]
                for n in range(len(xs))]

    r = [r_ref[0, 0, :, s].astype(F32) for s in lanes]
    k = [k_ref[0, 0, :, s].astype(F32) for s in lanes]
    v = [v_ref[0, 0, :, s].astype(F32) for s in lanes]
    lw = [lw_ref[0, :, s] for s in lanes]
    ai = [ai_ref[0, :, s] for s in lanes]

    kk = [k[p] * kk_ref[:, lanes[p]] for p in packs]
    ss = [seg_sums([kk[p] * kk[p]])[0] for p in packs]
    cum = []
    for p in packs:
        hi, lo = _split(lw[p])
        both = _dot(tri, jnp.concatenate([hi, lo], axis=1))
        cum.append(both[:, :L] + both[:, L:])

    k2, at, rt, bh, kh, g_last, v16, lhs, rhs = [], [], [], [], [], [], [], [], []
    for p in packs:
        kkn = kk[p] / jnp.maximum(jnp.sqrt(ss[p]), 1e-12)
        k2.append(k[p] * (1.0 + (ai[p] - 1.0) * ka_ref[:, lanes[p]]))
        b = kkn * ai[p]
        cum_last = cum[p][C - 1:C, :]
        g_inv = jnp.exp(-cum[p])
        g_tail = jnp.exp(cum_last - cum[p])
        at.append((-kkn * jnp.exp(cum[p] - lw[p])).astype(BF16))
        rt.append(r[p] * jnp.exp(cum[p]))
        bh.append((b * g_tail).astype(BF16))
        kh.append((k2[p] * g_tail).astype(BF16))
        g_last.append(jnp.exp(cum_last))
        v16.append(v[p].astype(BF16))
        lhs.append(jnp.concatenate([at[p], rt[p].astype(BF16)], axis=0))
        rhs.append(jnp.concatenate([bd((b * g_inv).astype(BF16)),
                                    bd((k2[p] * g_inv).astype(BF16))], axis=0))

    sbk = [_dot_nt(lhs[p], rhs[p]) for p in packs]
    a_ab = [jnp.where(strict, sbk[p][:C, :HP * C], 0.0) for p in packs]
    a_ak = [jnp.where(strict, sbk[p][:C, HP * C:], 0.0).astype(BF16) for p in packs]
    a_r = [jnp.concatenate([jnp.where(incl, sbk[p][C:, :HP * C], 0.0),
                            jnp.where(incl, sbk[p][C:, HP * C:], 0.0)], axis=1).astype(BF16)
           for p in packs]
    bdv = [bd(v16[p]) for p in packs]
    x1 = [_dot(a_ak[p], bdv[p]) for p in packs]

    mpow = a_ab
    tinv = [eye_packed + a_ab[p] for p in packs]
    n_sq = 1
    while n_sq * 2 < C:
        m16 = [mpow[p].astype(BF16) for p in packs]
        mpow = [_dot(m16[p], bd(m16[p])) for p in packs]
        step = [_dot(tinv[p].astype(BF16), bd(mpow[p].astype(BF16))) for p in packs]
        tinv = [tinv[p] + step[p] for p in packs]
        n_sq *= 2

    ua = [_dot(tinv[p].astype(BF16),
               jnp.concatenate([bd(x1[p].astype(BF16)), bd(at[p])], axis=1)) for p in packs]
    u0 = [ua[p][:, :L].astype(BF16) for p in packs]
    atp = [ua[p][:, L:].astype(BF16) for p in packs]
    zeros = jnp.zeros((HP * C, L), BF16)
    ry = [_dot(a_r[p], jnp.concatenate(
        [jnp.concatenate([bd(atp[p]), bd(u0[p])], axis=1),
         jnp.concatenate([zeros, bdv[p]], axis=1)], axis=0)) for p in packs]
    rp = [(rt[p] + ry[p][:, :L]).astype(BF16) for p in packs]
    y0 = [ry[p][:, L:] for p in packs]

    zc = jnp.zeros((C, L), BF16)
    pq = [_dot_tn(jnp.concatenate([jnp.concatenate([atp[p], u0[p]], axis=1),
                                   jnp.concatenate([zc, v16[p]], axis=1)], axis=0),
                  jnp.concatenate([bh[p], kh[p]], axis=0)) for p in packs]
    pt = [jnp.where(same_head, pq[p][:L], 0.0).astype(BF16) for p in packs]
    qt = [jnp.where(same_head, pq[p][L:], 0.0) for p in packs]

    s0 = [s_ref[p] for p in packs]
    s16 = [s0[p].astype(BF16) for p in packs]
    y = [_dot_nt(rp[p], s16[p]) + y0[p] for p in packs]
    for p in packs:
        s_ref[p] = s0[p] * g_last[p] + _dot(s16[p], pt[p]) + qt[p]

    st = [seg_sums([y[p], r[p] * k2[p] * rk_ref[:, lanes[p]]]) for p in packs]
    yc = [y[p] - st[p][0] * (1.0 / N) for p in packs]
    var = [seg_sums([yc[p] * yc[p]])[0] * (1.0 / N) for p in packs]
    for p in packs:
        yn = yc[p] * lax.rsqrt(var[p] + GN_EPS) * lnw_ref[:, lanes[p]] + lnb_ref[:, lanes[p]]
        gate = g_ref[0, 0, :, lanes[p]].astype(F32)
        out = (yn + st[p][1] * v[p]) * (gate / (1.0 + jnp.exp(-gate)))
        o_ref[0, :, lanes[p]] = out.astype(o_ref.dtype)


def _scan(rkvg, lw, ai, k_k, k_a, r_k, lnx_w, lnx_b):
    _, B, T, D = rkvg.shape
    C = SCAN_CHUNK
    W = _tile(D, SCAN_LANES, PACK_LANES)
    act = lambda n: pl.BlockSpec((1, 1, C, W), lambda b, g, c, n=n: (n, b, c, g))
    seq = pl.BlockSpec((1, C, W), lambda b, g, c: (b, c, g))
    row = pl.BlockSpec((1, W), lambda b, g, c: (0, g))
    return pl.pallas_call(
        _scan_kernel,
        out_shape=jax.ShapeDtypeStruct((B, T, D), BF16),
        grid=(B, D // W, T // C),
        in_specs=[act(0), act(1), act(2), act(3), seq, seq, row, row, row, row, row],
        out_specs=seq,
        scratch_shapes=[pltpu.VMEM((W // PACK_LANES, PACK_LANES, PACK_LANES), F32)],
        compiler_params=_params(("parallel", "parallel", "arbitrary")),
        name="rwkv_scan",
    )(rkvg, rkvg, rkvg, rkvg, lw, ai, k_k.reshape(1, D), k_a.reshape(1, D), r_k.reshape(1, D),
      lnx_w.reshape(1, D), lnx_b.reshape(1, D))


def _res_a_kernel(x_ref, mix_ref, gp_ref, gkv_ref, gb_ref, h_ref, hkv_ref, hb_ref):
    h = x_ref[...] + _rms(mix_ref[...], gp_ref[...])
    h_ref[...] = h
    hkv_ref[...] = _rms(h, gkv_ref[...]).astype(hkv_ref.dtype)
    hb_ref[...] = _rms(h, gb_ref[...]).astype(hb_ref.dtype)


def _res_a(x, mix, g_post, g_kv, g_b):
    M, D = x.shape
    tm = _tile(M, 128, 8)
    blk = pl.BlockSpec((tm, D), lambda i: (i, 0))
    row = pl.BlockSpec((1, D), lambda i: (0, 0))
    return pl.pallas_call(
        _res_a_kernel,
        out_shape=[jax.ShapeDtypeStruct((M, D), F32), jax.ShapeDtypeStruct((M, D), BF16),
                   jax.ShapeDtypeStruct((M, D), BF16)],
        grid=(M // tm,),
        in_specs=[blk, blk, row, row, row],
        out_specs=[blk, blk, blk],
        compiler_params=_params(("parallel",)),
        name="residual_a",
    )(x, mix, g_post.reshape(1, D), g_kv.reshape(1, D), g_b.reshape(1, D))


def _res_b_kernel(h_ref, mix_ref, g_ref, o_ref):
    o_ref[...] = h_ref[...] + _rms(mix_ref[...], g_ref[...])


def _res_b(h, mix, g):
    M, D = h.shape
    tm = _tile(M, 256, 8)
    blk = pl.BlockSpec((tm, D), lambda i: (i, 0))
    return pl.pallas_call(
        _res_b_kernel,
        out_shape=jax.ShapeDtypeStruct((M, D), F32),
        grid=(M // tm,),
        in_specs=[blk, blk, pl.BlockSpec((1, D), lambda i: (0, 0))],
        out_specs=blk,
        compiler_params=_params(("parallel",)),
        name="residual_b",
    )(h, mix, g.reshape(1, D))


def _attn_kernel(q_ref, gate_ref, k_ref, v_ref, o_ref, kb_ref, vt_ref, km_ref, sel_ref,
                 sa_ref, sb_ref, pa_ref, pb_ref, po_ref, top_ref, *, n_heads):
    BS = MOBA_BLOCK
    G = ATTN_GROUP
    QB = ATTN_QUERY_BLOCKS
    NQ = QB * BS
    assert G & (G - 1) == 0, "group size must be a power of two"
    T = k_ref.shape[1]
    NB = T // BS
    h = pl.program_id(1)
    step = pl.program_id(2)
    i = step * QB

    Dh = MOBA_HEAD_DIM
    assert BS >= Dh
    slope = LOG2_E * jnp.exp2(
        -8.0 * (jnp.zeros((1, NQ), F32) + (h + 1).astype(F32)) / n_heads)
    s_hi = slope.astype(BF16).astype(F32)
    s_lo = slope - s_hi

    @pl.when(step == 0)
    def _():
        kb_ref[:, :Dh] = k_ref[0].astype(BF16)
        lane = lax.broadcasted_iota(jnp.int32, (T, Dh), 1)
        key_in_block = (lax.broadcasted_iota(jnp.int32, (T, Dh), 0) % BS).astype(F32)
        extra = jnp.where(lane < 2, key_in_block,
                          jnp.where(lane == 2, s_hi[:, :Dh], jnp.where(lane == 3, s_lo[:, :Dh], 0.0)))
        kb_ref[:, Dh:] = extra.astype(BF16)
        pa_ref[...] = jnp.zeros_like(pa_ref)
        pb_ref[...] = jnp.zeros_like(pb_ref)
        for j in range(NB):
            rows = slice(j * BS, (j + 1) * BS)
            vt_ref[j, :Dh] = v_ref[0, rows, :].astype(F32).T.astype(BF16)
            vt_ref[j, Dh:] = jnp.ones((ONES_ROWS, BS), BF16)
            km_ref[j : j + 1, :] = jnp.mean(k_ref[0, rows, :].astype(F32), axis=0, keepdims=True)

    scale = LOG2_E * MOBA_HEAD_DIM ** -0.5
    q_t = q_ref[0].T
    sub = lax.broadcasted_iota(jnp.int32, (Dh, NQ), 0)
    neg_col = -(lax.broadcasted_iota(jnp.int32, (Dh, NQ), 1) % BS).astype(F32)
    q_extra = jnp.where(sub == 0, s_hi, jnp.where(sub == 1, s_lo, jnp.where(sub < 4, neg_col, 0.0)))
    q16 = jnp.concatenate([(q_t * scale).astype(BF16), q_extra.astype(BF16)], axis=0)
    q_half = lax.broadcasted_iota(jnp.int32, (1, NQ), 1) // BS

    km_hi, km_lo = _split(km_ref[...])
    qt_hi, qt_lo = _split(q_t)
    gate_s = _dot(jnp.concatenate([km_hi, km_hi, km_lo], axis=1),
                  jnp.concatenate([qt_hi, qt_lo, qt_hi], axis=0))
    blk = lax.broadcasted_iota(jnp.int32, gate_s.shape, 0)
    neg_inf = jnp.float32(-jnp.inf)
    gate_s = jnp.where(blk < i + lax.broadcasted_iota(jnp.int32, gate_s.shape, 1) // BS, gate_s,
                       neg_inf)
    sel = jnp.zeros(gate_s.shape, F32)
    for _ in range(min(MOBA_TOPK, NB)):
        top = jnp.max(gate_s, axis=0, keepdims=True)
        first = jnp.min(jnp.where(gate_s == top, blk, NB), axis=0, keepdims=True)
        hit = blk == first
        sel = jnp.where(hit & (top > neg_inf), 1.0, sel)
        gate_s = jnp.where(hit, neg_inf, gate_s)
    sel_ref[...] = sel

    diag = _dot(kb_ref[pl.ds(pl.multiple_of(i * BS, BS), QB * BS), :], q16)
    key_in = lax.broadcasted_iota(jnp.int32, (BS, NQ), 0)
    col = lax.broadcasted_iota(jnp.int32, (BS, NQ), 1)
    half_f = q_half.astype(F32)
    tops_own = []
    for u in range(QB):
        rows = slice(u * BS, (u + 1) * BS)
        visible = (col // BS > u) | ((col // BS == u) & (col % BS >= key_in))
        t = jnp.where(visible, diag[rows, :], neg_inf)
        c = jnp.max(t, axis=0, keepdims=True)
        seen = c > neg_inf
        po_ref[rows, :] = jnp.exp2(t - jnp.where(seen, c, 0.0)).astype(BF16)
        picked = jnp.where(half_f == u, 1.0, sel_ref[pl.ds(i + u, 1), :])
        shift = slope * (u - half_f) * BS
        tops_own.append(jnp.where(seen & (picked > 0.0), c + shift, neg_inf))
    top_seed = tops_own[0]
    for top in tops_own[1:]:
        top_seed = jnp.maximum(top_seed, top)
    acc0 = jnp.zeros((Dh + ONES_ROWS, NQ), F32)
    shift_own = slope * half_f * BS

    def issue_scores(k, s_ref):
        first = jnp.minimum(k * G, NB - G)
        s_ref[...] = _dot(kb_ref[pl.ds(pl.multiple_of(first * BS, BS), G * BS), :], q16)

    def normalise(k, s_ref, p_ref, stat_row):
        for u in range(G):
            jb = k * G + u
            in_past = (jb < i).astype(F32)
            keep = sel_ref[pl.ds(jnp.minimum(jb, NB - 1), 1), :] * in_past > 0.0
            shift = slope * ((jb - i) * BS).astype(F32) - shift_own
            rows = slice(u * BS, (u + 1) * BS)
            t = s_ref[rows, :]
            c = jnp.max(t, axis=0, keepdims=True)
            p_ref[rows, :] = jnp.exp2(t - c).astype(BF16)
            top_ref[stat_row + u:stat_row + u + 1, :] = jnp.where(keep, c + shift, neg_inf)

    def values(k, p_ref):
        return [_dot(vt_ref[jnp.clip(k * G + u, 0, NB - 1)], p_ref[u * BS:(u + 1) * BS, :])
                for u in range(G)]

    def read_stats():
        return [top_ref[n:n + 1, :] for n in range(2 * G)]

    def merge(outs, tops, state):
        m, acc = state
        m_new = m
        for top in tops:
            m_new = jnp.maximum(m_new, top)
        acc = acc * jnp.exp2(m - m_new)
        for u in range(len(outs)):
            acc = acc + outs[u] * jnp.exp2(tops[u] - m_new)
        return m_new, acc

    def two_groups(kk, state):
        k = 2 * kk
        stats = read_stats()
        outs = values(k - 2, pa_ref) + values(k - 1, pb_ref)
        normalise(k, sa_ref, pa_ref, 0)
        issue_scores(k + 2, sa_ref)
        normalise(k + 1, sb_ref, pb_ref, G)
        issue_scores(k + 3, sb_ref)
        return merge(outs, stats, state)

    issue_scores(0, sa_ref)
    issue_scores(1, sb_ref)
    top_ref[...] = jnp.full(top_ref.shape, neg_inf, F32)
    n_passes = lax.shift_right_logical(i + (2 * G - 1), jnp.int32((2 * G).bit_length() - 1))
    state = lax.fori_loop(0, n_passes, two_groups, (top_seed, acc0))
    last = 2 * n_passes
    outs = (values(last - 2, pa_ref) + values(last - 1, pb_ref)
            + [_dot(vt_ref[i + u], po_ref[u * BS:(u + 1) * BS, :]) for u in range(QB)])
    m, acc = merge(outs, read_stats() + tops_own, state)

    att = (acc[:Dh] / acc[Dh:Dh + 1]).T
    gate = gate_ref[0]
    o_ref[0] = (att * (gate / (1.0 + jnp.exp(-gate)))).astype(o_ref.dtype)


def _attention(qg, kv, n_heads):
    B, T, D2 = qg.shape
    D = D2 // 2
    BS, Dh = MOBA_BLOCK, MOBA_HEAD_DIM
    NB = T // BS
    QB = ATTN_QUERY_BLOCKS
    NQ = QB * BS
    assert NB % QB == 0 and NB % ATTN_GROUP == 0
    return pl.pallas_call(
        functools.partial(_attn_kernel, n_heads=n_heads),
        out_shape=jax.ShapeDtypeStruct((B, T, D), BF16),
        grid=(B, n_heads, NB // QB),
        in_specs=[
            pl.BlockSpec((1, NQ, Dh), lambda b, h, i: (b, i, h)),
            pl.BlockSpec((1, NQ, Dh), lambda b, h, i: (b, i, n_heads + h)),
            pl.BlockSpec((1, T, Dh), lambda b, h, i: (b, 0, h)),
            pl.BlockSpec((1, T, Dh), lambda b, h, i: (b, 0, n_heads + h)),
        ],
        out_specs=pl.BlockSpec((1, NQ, Dh), lambda b, h, i: (b, i, h)),
        scratch_shapes=[
            pltpu.VMEM((T, 2 * Dh), BF16),
            pltpu.VMEM((NB, Dh + ONES_ROWS, BS), BF16),
            pltpu.VMEM((NB, Dh), F32),
            pltpu.VMEM((NB, NQ), F32),
            pltpu.VMEM((ATTN_GROUP * BS, NQ), F32),
            pltpu.VMEM((ATTN_GROUP * BS, NQ), F32),
            pltpu.VMEM((ATTN_GROUP * BS, NQ), BF16),
            pltpu.VMEM((ATTN_GROUP * BS, NQ), BF16),
            pltpu.VMEM((QB * BS, NQ), BF16),
            pltpu.VMEM((2 * ATTN_GROUP, NQ), F32),
        ],
        compiler_params=_params(("parallel", "parallel", "arbitrary")),
        name="moba_attention",
    )(qg, qg, kv, kv)


def kernel(x, a_pre_g, a_post_g, a_mu, a_w_in, a_w0, a_w1, a_w2, a_a0, a_a1, a_a2, a_k_k, a_k_a,
           a_r_k, a_lnx_w, a_lnx_b, a_w_o, kv_norm_g, w_k, w_v, b_pre_g, b_post_g, b_w_qg, b_w_o):
    B, T, D = x.shape
    M = B * T
    assert a_pre_g.shape[0] == 1 and b_pre_g.shape[0] == 1, "one RWKV layer then one MoBA layer"
    assert T % MOBA_BLOCK == 0 and D % LANES == 0 and T % SCAN_CHUNK == 0
    n_moba_heads = D // MOBA_HEAD_DIM

    mixes = _prep(x, a_pre_g[0], a_mu[0]).reshape(a_mu.shape[1], M, D)
    rkvg = _matmul(mixes, a_w_in[0].astype(BF16), BF16, name="rwkv_proj")
    w_lora = jnp.stack([a_w1[0], a_a1[0]]).astype(BF16)
    lo = _matmul(mixes, w_lora, F32, x_off=4, name="rwkv_lora")
    lw, ai = _rates(lo, a_w0[0], a_w2[0].astype(BF16), a_a0[0], a_a2[0].astype(BF16))
    yg = _scan(rkvg.reshape(4, B, T, D), lw.reshape(B, T, D), ai.reshape(B, T, D), a_k_k[0],
               a_k_a[0], a_r_k[0], a_lnx_w[0], a_lnx_b[0])
    mix_a = _matmul(yg.reshape(1, M, D), a_w_o[0].astype(BF16)[None], F32, name="rwkv_out")[0]
    h1, hkv, hb = _res_a(x.reshape(M, D), mix_a, a_post_g[0], kv_norm_g, b_pre_g[0])

    w_kv = jnp.concatenate([w_k, w_v], axis=1).astype(BF16)[None]
    kv = _matmul(hkv[None], w_kv, BF16, name="kv_proj")[0]
    qg = _matmul(hb[None], b_w_qg[0].astype(BF16)[None], F32, name="qg_proj")[0]
    yb = _attention(qg.reshape(B, T, 2 * D), kv.reshape(B, T, 2 * D), n_moba_heads)
    mix_b = _matmul(yb.reshape(1, M, D), b_w_o[0].astype(BF16)[None], F32, name="moba_out")[0]
    return _res_b(h1, mix_b, b_post_g[0]).reshape(B, T, D)
```

```python
import functools

import jax
import jax.numpy as jnp
from jax import lax
from jax.experimental import pallas as pl
from jax.experimental.pallas import tpu as pltpu

RWKV_HEAD_DIM = 64
MOBA_HEAD_DIM = 128
MOBA_BLOCK = 256
MOBA_TOPK = 3
GN_EPS = 64e-5
NORM_EPS = 1e-6

LANES = 128
MXU_WIDTH = 256
SCAN_CHUNK = RWKV_HEAD_DIM
PACK_LANES = MXU_WIDTH
HEADS_PER_PACK = PACK_LANES // RWKV_HEAD_DIM
SCAN_LANES = 2048
PREP_COLS = 256
ONES_ROWS = 16
LOG2_E = 1.4426950408889634
EXP_NEG_HALF = 0.6065306597126334
ATTN_QUERY_BLOCKS = 2
ATTN_GROUP = 2
VMEM_LIMIT = 56 * 1024 * 1024

F32 = jnp.float32
BF16 = jnp.bfloat16


def _tile(n, pref, align):
    if n <= pref:
        return n
    t = (pref // align) * align
    while t >= align:
        if n % t == 0:
            return t
        t -= align
    return n


def _params(sem):
    return pltpu.CompilerParams(dimension_semantics=sem, vmem_limit_bytes=VMEM_LIMIT)


def _rms(x, g):
    return x * lax.rsqrt(jnp.mean(x * x, axis=-1, keepdims=True) + NORM_EPS) * g


def _dot(a, b):
    return jnp.dot(a, b, preferred_element_type=F32)


def _dot_nt(a, b):
    return lax.dot_general(a, b, (((1,), (1,)), ((), ())), preferred_element_type=F32)


def _dot_tn(a, b):
    return lax.dot_general(a, b, (((0,), (0,)), ((), ())), preferred_element_type=F32)


def _split(x):
    hi = x.astype(BF16)
    lo = (x - hi.astype(F32)).astype(BF16)
    return hi, lo


def _prep_kernel(x_ref, xp_ref, g_ref, mu_ref, o_ref):
    i = pl.program_id(1)
    tt, D = x_ref.shape[1:]
    x = x_ref[0]
    inv = lax.rsqrt(jnp.mean(x * x, axis=-1, keepdims=True) + NORM_EPS)
    xp = xp_ref[0, 7:8, :]
    inv_p = lax.rsqrt(jnp.mean(xp * xp, axis=-1, keepdims=True) + NORM_EPS)
    inv_p = jnp.where(i == 0, 0.0, inv_p)
    W = PREP_COLS
    row = lax.broadcasted_iota(jnp.int32, (tt, W), 0)
    for c in range(D // W):
        cols = slice(c * W, (c + 1) * W)
        g = g_ref[:, cols]
        hn = x_ref[0, :, cols] * inv * g
        prev = xp_ref[0, 7:8, cols] * inv_p * g
        shifted = jnp.where(row == 0, prev, pltpu.roll(hn, shift=1, axis=0))
        dx = shifted - hn
        for n in range(o_ref.shape[0]):
            o_ref[n, 0, :, cols] = (hn + dx * mu_ref[n : n + 1, cols]).astype(o_ref.dtype)


def _prep(x, g, mu):
    B, T, D = x.shape
    n_mix = mu.shape[0]
    tt = _tile(T, 128, 8)
    return pl.pallas_call(
        _prep_kernel,
        out_shape=jax.ShapeDtypeStruct((n_mix, B, T, D), BF16),
        grid=(B, T // tt),
        in_specs=[
            pl.BlockSpec((1, tt, D), lambda b, i: (b, i, 0)),
            pl.BlockSpec((1, 8, D), lambda b, i: (b, jnp.maximum(i * (tt // 8) - 1, 0), 0)),
            pl.BlockSpec((1, D), lambda b, i: (0, 0)),
            pl.BlockSpec((n_mix, D), lambda b, i: (0, 0)),
        ],
        out_specs=pl.BlockSpec((n_mix, 1, tt, D), lambda b, i: (0, b, i, 0)),
        compiler_params=_params(("parallel", "parallel")),
        name="rwkv_prep",
    )(x, x, g.reshape(1, D), mu)


def _mm_kernel(x_ref, w_ref, o_ref):
    o_ref[0] = _dot(x_ref[0], w_ref[0]).astype(o_ref.dtype)


def _matmul(x, w, out_dtype, *, x_off=0, name="matmul"):
    G, K, N = w.shape
    M = x.shape[1]
    tm = _tile(M, 1024, 8)
    tn = _tile(N, 512, LANES)
    return pl.pallas_call(
        _mm_kernel,
        out_shape=jax.ShapeDtypeStruct((G, M, N), out_dtype),
        grid=(G, M // tm, N // tn),
        in_specs=[
            pl.BlockSpec((1, tm, K), lambda g, i, j: (g + x_off, i, 0)),
            pl.BlockSpec((1, K, tn), lambda g, i, j: (g, 0, j)),
        ],
        out_specs=pl.BlockSpec((1, tm, tn), lambda g, i, j: (g, i, j)),
        compiler_params=_params(("parallel", "parallel", "arbitrary")),
        name=name,
    )(x, w)


def _rates_kernel(lo_ref, w0_ref, w2_ref, a0_ref, a2_ref, lw_ref, ai_ref):
    z = w0_ref[...] + _dot(jnp.tanh(lo_ref[0]).astype(BF16), w2_ref[...])
    lw_ref[...] = -EXP_NEG_HALF / (1.0 + jnp.exp(-z))
    y = a0_ref[...] + _dot(lo_ref[1].astype(BF16), a2_ref[...])
    ai_ref[...] = (1.0 / (1.0 + jnp.exp(-y))).astype(ai_ref.dtype)


def _rates(lo, w0, w2, a0, a2):
    _, M, R = lo.shape
    D = w2.shape[1]
    tm = _tile(M, 256, 8)
    row = pl.BlockSpec((1, D), lambda i: (0, 0))
    mat = pl.BlockSpec((R, D), lambda i: (0, 0))
    out = pl.BlockSpec((tm, D), lambda i: (i, 0))
    return pl.pallas_call(
        _rates_kernel,
        out_shape=[jax.ShapeDtypeStruct((M, D), F32), jax.ShapeDtypeStruct((M, D), BF16)],
        grid=(M // tm,),
        in_specs=[pl.BlockSpec((2, tm, R), lambda i: (0, i, 0)), row, mat, row, mat],
        out_specs=[out, out],
        compiler_params=_params(("parallel",)),
        name="rwkv_rates",
    )(lo, w0.reshape(1, D), w2, a0.reshape(1, D), a2)


def _scan_kernel(r_ref, k_ref, v_ref, g_ref, lw_ref, ai_ref, kk_ref, ka_ref, rk_ref,
                 lnw_ref, lnb_ref, o_ref, s_ref):
    C = SCAN_CHUNK
    N = RWKV_HEAD_DIM
    L = PACK_LANES
    HP = HEADS_PER_PACK
    packs = range(o_ref.shape[2] // L)
    lanes = [slice(p * L, (p + 1) * L) for p in packs]

    @pl.when(pl.program_id(2) == 0)
    def _():
        s_ref[...] = jnp.zeros_like(s_ref)

    t_row = lax.broadcasted_iota(jnp.int32, (C, HP * C), 0)
    s_col = lax.broadcasted_iota(jnp.int32, (C, HP * C), 1) % C
    strict = s_col < t_row
    incl = s_col <= t_row
    eye_packed = (s_col == t_row).astype(F32)
    tri = (lax.broadcasted_iota(jnp.int32, (C, C), 1)
           <= lax.broadcasted_iota(jnp.int32, (C, C), 0)).astype(BF16)
    same_head = ((lax.broadcasted_iota(jnp.int32, (L, L), 0) // N)
                 == (lax.broadcasted_iota(jnp.int32, (L, L), 1) // N))
    ones_bd = same_head.astype(BF16)
    lane_head = lax.broadcasted_iota(jnp.int32, (C, L), 1) // N

    def bd(x):
        return jnp.concatenate(
            [jnp.where(lane_head == h, x, jnp.zeros_like(x)) for h in range(HP)], axis=0)

    def seg_sums(xs):
        parts = []
        for x in xs:
            parts.extend(_split(x))
        res = _dot(jnp.concatenate(parts, axis=0), ones_bd)
        return [res[2 * n * C:(2 * n + 1) * C] + res[(2 * n + 1) * C:(2 * n + 2) * C]
                for n in range(len(xs))]

    r = [r_ref[0, 0, :, s].astype(F32) for s in lanes]
    k = [k_ref[0, 0, :, s].astype(F32) for s in lanes]
    v = [v_ref[0, 0, :, s].astype(F32) for s in lanes]
    lw = [lw_ref[0, :, s] for s in lanes]
    ai = [ai_ref[0, :, s].astype(F32) for s in lanes]

    kk = [k[p] * kk_ref[:, lanes[p]] for p in packs]
    ss = [seg_sums([kk[p] * kk[p]])[0] for p in packs]
    cum = []
    for p in packs:
        hi, lo = _split(lw[p])
        both = _dot(tri, jnp.concatenate([hi, lo], axis=1))
        cum.append(both[:, :L] + both[:, L:])

    k2, at, rt, bh, kh, g_last, v16, lhs, rhs = [], [], [], [], [], [], [], [], []
    for p in packs:
        kkn = kk[p] / jnp.maximum(jnp.sqrt(ss[p]), 1e-12)
        k2.append(k[p] * (1.0 + (ai[p] - 1.0) * ka_ref[:, lanes[p]]))
        b = kkn * ai[p]
        cum_last = cum[p][C - 1:C, :]
        g_inv = jnp.exp(-cum[p])
        g_tail = jnp.exp(cum_last - cum[p])
        at.append((-kkn * jnp.exp(cum[p] - lw[p])).astype(BF16))
        rt.append(r[p] * jnp.exp(cum[p]))
        bh.append((b * g_tail).astype(BF16))
        kh.append((k2[p] * g_tail).astype(BF16))
        g_last.append(jnp.exp(cum_last))
        v16.append(v[p].astype(BF16))
        lhs.append(jnp.concatenate([at[p], rt[p].astype(BF16)], axis=0))
        rhs.append(jnp.concatenate([bd((b * g_inv).astype(BF16)),
                                    bd((k2[p] * g_inv).astype(BF16))], axis=0))

    sbk = [_dot_nt(lhs[p], rhs[p]) for p in packs]
    a_ab = [jnp.where(strict, sbk[p][:C, :HP * C], 0.0) for p in packs]
    a_ak = [jnp.where(strict, sbk[p][:C, HP * C:], 0.0).astype(BF16) for p in packs]
    a_r = [jnp.concatenate([jnp.where(incl, sbk[p][C:, :HP * C], 0.0),
                            jnp.where(incl, sbk[p][C:, HP * C:], 0.0)], axis=1).astype(BF16)
           for p in packs]
    bdv = [bd(v16[p]) for p in packs]
    x1 = [_dot(a_ak[p], bdv[p]) for p in packs]

    mpow = a_ab
    tinv = [eye_packed + a_ab[p] for p in packs]
    n_sq = 1
    while n_sq * 2 < C:
        m16 = [mpow[p].astype(BF16) for p in packs]
        mpow = [_dot(m16[p], bd(m16[p])) for p in packs]
        step = [_dot(tinv[p].astype(BF16), bd(mpow[p].astype(BF16))) for p in packs]
        tinv = [tinv[p] + step[p] for p in packs]
        n_sq *= 2

    ua = [_dot(tinv[p].astype(BF16),
               jnp.concatenate([bd(x1[p].astype(BF16)), bd(at[p])], axis=1)) for p in packs]
    u0 = [ua[p][:, :L].astype(BF16) for p in packs]
    atp = [ua[p][:, L:].astype(BF16) for p in packs]
    zeros = jnp.zeros((HP * C, L), BF16)
    ry = [_dot(a_r[p], jnp.concatenate(
        [jnp.concatenate([bd(atp[p]), bd(u0[p])], axis=1),
         jnp.concatenate([zeros, bdv[p]], axis=1)], axis=0)) for p in packs]
    rp = [(rt[p] + ry[p][:, :L]).astype(BF16) for p in packs]
    y0 = [ry[p][:, L:] for p in packs]

    zc = jnp.zeros((C, L), BF16)
    pq = [_dot_tn(jnp.concatenate([jnp.concatenate([atp[p], u0[p]], axis=1),
                                   jnp.concatenate([zc, v16[p]], axis=1)], axis=0),
                  jnp.concatenate([bh[p], kh[p]], axis=0)) for p in packs]
    pt = [jnp.where(same_head, pq[p][:L], 0.0).astype(BF16) for p in packs]
    qt = [jnp.where(same_head, pq[p][L:], 0.0) for p in packs]

    s0 = [s_ref[p] for p in packs]
    s16 = [s0[p].astype(BF16) for p in packs]
    y = [_dot_nt(rp[p], s16[p]) + y0[p] for p in packs]
    for p in packs:
        s_ref[p] = s0[p] * g_last[p] + _dot(s16[p], pt[p]) + qt[p]

    st = [seg_sums([y[p], r[p] * k2[p] * rk_ref[:, lanes[p]]]) for p in packs]
    yc = [y[p] - st[p][0] * (1.0 / N) for p in packs]
    var = [seg_sums([yc[p] * yc[p]])[0] * (1.0 / N) for p in packs]
    for p in packs:
        yn = yc[p] * lax.rsqrt(var[p] + GN_EPS) * lnw_ref[:, lanes[p]] + lnb_ref[:, lanes[p]]
        gate = g_ref[0, 0, :, lanes[p]].astype(F32)
        out = (yn + st[p][1] * v[p]) * (gate / (1.0 + jnp.exp(-gate)))
        o_ref[0, :, lanes[p]] = out.astype(o_ref.dtype)


def _scan(rkvg, lw, ai, k_k, k_a, r_k, lnx_w, lnx_b):
    _, B, T, D = rkvg.shape
    C = SCAN_CHUNK
    W = _tile(D, SCAN_LANES, PACK_LANES)
    act = lambda n: pl.BlockSpec((1, 1, C, W), lambda b, g, c, n=n: (n, b, c, g))
    seq = pl.BlockSpec((1, C, W), lambda b, g, c: (b, c, g))
    row = pl.BlockSpec((1, W), lambda b, g, c: (0, g))
    return pl.pallas_call(
        _scan_kernel,
        out_shape=jax.ShapeDtypeStruct((B, T, D), BF16),
        grid=(B, D // W, T // C),
        in_specs=[act(0), act(1), act(2), act(3), seq, seq, row, row, row, row, row],
        out_specs=seq,
        scratch_shapes=[pltpu.VMEM((W // PACK_LANES, PACK_LANES, PACK_LANES), F32)],
        compiler_params=_params(("parallel", "parallel", "arbitrary")),
        name="rwkv_scan",
    )(rkvg, rkvg, rkvg, rkvg, lw, ai, k_k.reshape(1, D), k_a.reshape(1, D), r_k.reshape(1, D),
      lnx_w.reshape(1, D), lnx_b.reshape(1, D))


def _res_a_kernel(x_ref, mix_ref, gp_ref, gkv_ref, gb_ref, h_ref, hkv_ref, hb_ref):
    h = x_ref[...] + _rms(mix_ref[...], gp_ref[...])
    h_ref[...] = h
    hkv_ref[...] = _rms(h, gkv_ref[...]).astype(hkv_ref.dtype)
    hb_ref[...] = _rms(h, gb_ref[...]).astype(hb_ref.dtype)


def _res_a(x, mix, g_post, g_kv, g_b):
    M, D = x.shape
    tm = _tile(M, 128, 8)
    blk = pl.BlockSpec((tm, D), lambda i: (i, 0))
    row = pl.BlockSpec((1, D), lambda i: (0, 0))
    return pl.pallas_call(
        _res_a_kernel,
        out_shape=[jax.ShapeDtypeStruct((M, D), F32), jax.ShapeDtypeStruct((M, D), BF16),
                   jax.ShapeDtypeStruct((M, D), BF16)],
        grid=(M // tm,),
        in_specs=[blk, blk, row, row, row],
        out_specs=[blk, blk, blk],
        compiler_params=_params(("parallel",)),
        name="residual_a",
    )(x, mix, g_post.reshape(1, D), g_kv.reshape(1, D), g_b.reshape(1, D))


def _res_b_kernel(h_ref, mix_ref, g_ref, o_ref):
    o_ref[...] = h_ref[...] + _rms(mix_ref[...], g_ref[...])


def _res_b(h, mix, g):
    M, D = h.shape
    tm = _tile(M, 256, 8)
    blk = pl.BlockSpec((tm, D), lambda i: (i, 0))
    return pl.pallas_call(
        _res_b_kernel,
        out_shape=jax.ShapeDtypeStruct((M, D), F32),
        grid=(M // tm,),
        in_specs=[blk, blk, pl.BlockSpec((1, D), lambda i: (0, 0))],
        out_specs=blk,
        compiler_params=_params(("parallel",)),
        name="residual_b",
    )(h, mix, g.reshape(1, D))


def _attn_kernel(q_ref, gate_ref, k_ref, v_ref, o_ref, kb_ref, vt_ref, km_ref, sel_ref,
                 sa_ref, sb_ref, pa_ref, pb_ref, po_ref, top_ref, *, n_heads):
    BS = MOBA_BLOCK
    G = ATTN_GROUP
    QB = ATTN_QUERY_BLOCKS
    NQ = QB * BS
    assert G & (G - 1) == 0, "group size must be a power of two"
    T = k_ref.shape[1]
    NB = T // BS
    h = pl.program_id(1)
    step = pl.program_id(2)
    i = step * QB

    Dh = MOBA_HEAD_DIM
    assert BS >= Dh
    slope = LOG2_E * jnp.exp2(
        -8.0 * (jnp.zeros((1, NQ), F32) + (h + 1).astype(F32)) / n_heads)
    s_hi = slope.astype(BF16).astype(F32)
    s_lo = slope - s_hi

    @pl.when(step == 0)
    def _():
        kb_ref[:, :Dh] = k_ref[0].astype(BF16)
        lane = lax.broadcasted_iota(jnp.int32, (T, Dh), 1)
        key_in_block = (lax.broadcasted_iota(jnp.int32, (T, Dh), 0) % BS).astype(F32)
        extra = jnp.where(lane < 2, key_in_block,
                          jnp.where(lane == 2, s_hi[:, :Dh], jnp.where(lane == 3, s_lo[:, :Dh], 0.0)))
        kb_ref[:, Dh:] = extra.astype(BF16)
        pa_ref[...] = jnp.zeros_like(pa_ref)
        pb_ref[...] = jnp.zeros_like(pb_ref)
        for j in range(NB):
            rows = slice(j * BS, (j + 1) * BS)
            vt_ref[j, :Dh] = v_ref[0, rows, :].astype(F32).T.astype(BF16)
            vt_ref[j, Dh:] = jnp.ones((ONES_ROWS, BS), BF16)
            km_ref[j : j + 1, :] = jnp.mean(k_ref[0, rows, :].astype(F32), axis=0, keepdims=True)

    scale = LOG2_E * MOBA_HEAD_DIM ** -0.5
    q_t = q_ref[0].T
    sub = lax.broadcasted_iota(jnp.int32, (Dh, NQ), 0)
    neg_col = -(lax.broadcasted_iota(jnp.int32, (Dh, NQ), 1) % BS).astype(F32)
    q_extra = jnp.where(sub == 0, s_hi, jnp.where(sub == 1, s_lo, jnp.where(sub < 4, neg_col, 0.0)))
    q16 = jnp.concatenate([(q_t * scale).astype(BF16), q_extra.astype(BF16)], axis=0)
    q_half = lax.broadcasted_iota(jnp.int32, (1, NQ), 1) // BS

    km_hi, km_lo = _split(km_ref[...])
    qt_hi, qt_lo = _split(q_t)
    gate_s = _dot(jnp.concatenate([km_hi, km_hi, km_lo], axis=1),
                  jnp.concatenate([qt_hi, qt_lo, qt_hi], axis=0))
    blk = lax.broadcasted_iota(jnp.int32, gate_s.shape, 0)
    neg_inf = jnp.float32(-jnp.inf)
    gate_s = jnp.where(blk < i + lax.broadcasted_iota(jnp.int32, gate_s.shape, 1) // BS, gate_s,
                       neg_inf)
    sel = jnp.zeros(gate_s.shape, F32)
    for _ in range(min(MOBA_TOPK, NB)):
        top = jnp.max(gate_s, axis=0, keepdims=True)
        first = jnp.min(jnp.where(gate_s == top, blk, NB), axis=0, keepdims=True)
        hit = blk == first
        sel = jnp.where(hit & (top > neg_inf), 1.0, sel)
        gate_s = jnp.where(hit, neg_inf, gate_s)
    sel_ref[...] = sel

    diag = _dot(kb_ref[pl.ds(pl.multiple_of(i * BS, BS), QB * BS), :], q16)
    key_in = lax.broadcasted_iota(jnp.int32, (BS, NQ), 0)
    col = lax.broadcasted_iota(jnp.int32, (BS, NQ), 1)
    half_f = q_half.astype(F32)
    tops_own = []
    for u in range(QB):
        rows = slice(u * BS, (u + 1) * BS)
        visible = (col // BS > u) | ((col // BS == u) & (col % BS >= key_in))
        t = jnp.where(visible, diag[rows, :], neg_inf)
        c = jnp.max(t, axis=0, keepdims=True)
        seen = c > neg_inf
        po_ref[rows, :] = jnp.exp2(t - jnp.where(seen, c, 0.0)).astype(BF16)
        picked = jnp.where(half_f == u, 1.0, sel_ref[pl.ds(i + u, 1), :])
        shift = slope * (u - half_f) * BS
        tops_own.append(jnp.where(seen & (picked > 0.0), c + shift, neg_inf))
    top_seed = tops_own[0]
    for top in tops_own[1:]:
        top_seed = jnp.maximum(top_seed, top)
    acc0 = jnp.zeros((Dh + ONES_ROWS, NQ), F32)
    shift_own = slope * half_f * BS

    def issue_scores(k, s_ref):
        first = jnp.minimum(k * G, NB - G)
        s_ref[...] = _dot(kb_ref[pl.ds(pl.multiple_of(first * BS, BS), G * BS), :], q16)

    def normalise(k, s_ref, p_ref, stat_row):
        for u in range(G):
            jb = k * G + u
            in_past = (jb < i).astype(F32)
            keep = sel_ref[pl.ds(jnp.minimum(jb, NB - 1), 1), :] * in_past > 0.0
            shift = slope * ((jb - i) * BS).astype(F32) - shift_own
            rows = slice(u * BS, (u + 1) * BS)
            t = s_ref[rows, :]
            c = jnp.max(t, axis=0, keepdims=True)
            p_ref[rows, :] = jnp.exp2(t - c).astype(BF16)
            top_ref[stat_row + u:stat_row + u + 1, :] = jnp.where(keep, c + shift, neg_inf)

    def values(k, p_ref):
        return [_dot(vt_ref[jnp.clip(k * G + u, 0, NB - 1)], p_ref[u * BS:(u + 1) * BS, :])
                for u in range(G)]

    def read_stats():
        return [top_ref[n:n + 1, :] for n in range(2 * G)]

    def merge(outs, tops, state):
        m, acc = state
        m_new = m
        for top in tops:
            m_new = jnp.maximum(m_new, top)
        acc = acc * jnp.exp2(m - m_new)
        for u in range(len(outs)):
            acc = acc + outs[u] * jnp.exp2(tops[u] - m_new)
        return m_new, acc

    def two_groups(kk, state):
        k = 2 * kk
        stats = read_stats()
        outs = values(k - 2, pa_ref) + values(k - 1, pb_ref)
        normalise(k, sa_ref, pa_ref, 0)
        issue_scores(k + 2, sa_ref)
        normalise(k + 1, sb_ref, pb_ref, G)
        issue_scores(k + 3, sb_ref)
        return merge(outs, stats, state)

    issue_scores(0, sa_ref)
    issue_scores(1, sb_ref)
    top_ref[...] = jnp.full(top_ref.shape, neg_inf, F32)
    n_passes = lax.shift_right_logical(i + (2 * G - 1), jnp.int32((2 * G).bit_length() - 1))
    state = lax.fori_loop(0, n_passes, two_groups, (top_seed, acc0))
    last = 2 * n_passes
    outs = (values(last - 2, pa_ref) + values(last - 1, pb_ref)
            + [_dot(vt_ref[i + u], po_ref[u * BS:(u + 1) * BS, :]) for u in range(QB)])
    m, acc = merge(outs, read_stats() + tops_own, state)

    att = (acc[:Dh] / acc[Dh:Dh + 1]).T
    gate = gate_ref[0]
    o_ref[0] = (att * (gate / (1.0 + jnp.exp(-gate)))).astype(o_ref.dtype)


def _attention(qg, kv, n_heads):
    B, T, D2 = qg.shape
    D = D2 // 2
    BS, Dh = MOBA_BLOCK, MOBA_HEAD_DIM
    NB = T // BS
    QB = ATTN_QUERY_BLOCKS
    NQ = QB * BS
    assert NB % QB == 0 and NB % ATTN_GROUP == 0
    return pl.pallas_call(
        functools.partial(_attn_kernel, n_heads=n_heads),
        out_shape=jax.ShapeDtypeStruct((B, T, D), BF16),
        grid=(B, n_heads, NB // QB),
        in_specs=[
            pl.BlockSpec((1, NQ, Dh), lambda b, h, i: (b, i, h)),
            pl.BlockSpec((1, NQ, Dh), lambda b, h, i: (b, i, n_heads + h)),
            pl.BlockSpec((1, T, Dh), lambda b, h, i: (b, 0, h)),
            pl.BlockSpec((1, T, Dh), lambda b, h, i: (b, 0, n_heads + h)),
        ],
        out_specs=pl.BlockSpec((1, NQ, Dh), lambda b, h, i: (b, i, h)),
        scratch_shapes=[
            pltpu.VMEM((T, 2 * Dh), BF16),
            pltpu.VMEM((NB, Dh + ONES_ROWS, BS), BF16),
            pltpu.VMEM((NB, Dh), F32),
            pltpu.VMEM((NB, NQ), F32),
            pltpu.VMEM((ATTN_GROUP * BS, NQ), F32),
            pltpu.VMEM((ATTN_GROUP * BS, NQ), F32),
            pltpu.VMEM((ATTN_GROUP * BS, NQ), BF16),
            pltpu.VMEM((ATTN_GROUP * BS, NQ), BF16),
            pltpu.VMEM((QB * BS, NQ), BF16),
            pltpu.VMEM((2 * ATTN_GROUP, NQ), F32),
        ],
        compiler_params=_params(("parallel", "parallel", "arbitrary")),
        name="moba_attention",
    )(qg, qg, kv, kv)


def kernel(x, a_pre_g, a_post_g, a_mu, a_w_in, a_w0, a_w1, a_w2, a_a0, a_a1, a_a2, a_k_k, a_k_a,
           a_r_k, a_lnx_w, a_lnx_b, a_w_o, kv_norm_g, w_k, w_v, b_pre_g, b_post_g, b_w_qg, b_w_o):
    B, T, D = x.shape
    M = B * T
    assert a_pre_g.shape[0] == 1 and b_pre_g.shape[0] == 1, "one RWKV layer then one MoBA layer"
    assert T % MOBA_BLOCK == 0 and D % LANES == 0 and T % SCAN_CHUNK == 0
    n_moba_heads = D // MOBA_HEAD_DIM

    mixes = _prep(x, a_pre_g[0], a_mu[0]).reshape(a_mu.shape[1], M, D)
    rkvg = _matmul(mixes, a_w_in[0].astype(BF16), BF16, name="rwkv_proj")
    w_lora = jnp.stack([a_w1[0], a_a1[0]]).astype(BF16)
    lo = _matmul(mixes, w_lora, F32, x_off=4, name="rwkv_lora")
    lw, ai = _rates(lo, a_w0[0], a_w2[0].astype(BF16), a_a0[0], a_a2[0].astype(BF16))
    yg = _scan(rkvg.reshape(4, B, T, D), lw.reshape(B, T, D), ai.reshape(B, T, D), a_k_k[0],
               a_k_a[0], a_r_k[0], a_lnx_w[0], a_lnx_b[0])
    mix_a = _matmul(yg.reshape(1, M, D), a_w_o[0].astype(BF16)[None], F32, name="rwkv_out")[0]
    h1, hkv, hb = _res_a(x.reshape(M, D), mix_a, a_post_g[0], kv_norm_g, b_pre_g[0])

    w_kv = jnp.concatenate([w_k, w_v], axis=1).astype(BF16)[None]
    kv = _matmul(hkv[None], w_kv, BF16, name="kv_proj")[0]
    qg = _matmul(hb[None], b_w_qg[0].astype(BF16)[None], F32, name="qg_proj")[0]
    yb = _attention(qg.reshape(B, T, 2 * D), kv.reshape(B, T, 2 * D), n_moba_heads)
    mix_b = _matmul(yb.reshape(1, M, D), b_w_o[0].astype(BF16)[None], F32, name="moba_out")[0]
    return _res_b(h1, mix_b, b_post_g[0]).reshape(B, T, D)
```
